```python
import math
import jax
import jax.numpy as jnp
from jax import lax
import numpy as np

D_MODEL = 1024
BATCH = 8
SEQ = 4096
DEPTH = 2

GRID_W = 64
CTX_LEN = 256
N_SUB = 3
N_MOD = 3 * N_SUB
D_FF = 2816
FFN_RES = 0.5
NORM_EPS = 1e-6

CHUNK = 128
D_A = 768
A_GROUPS = 6
A_GROUP_DIM = D_A // A_GROUPS

D_B = D_MODEL - D_A
S5_GROUP = 16
S5_GROUPS = D_B // S5_GROUP
S5_STATE = 64
N_DIR = 2

N_HEADS = 8
N_KV_HEADS = 2
HEAD_DIM = D_MODEL // N_HEADS
Q_PER_KV = N_HEADS // N_KV_HEADS
Q_DIM = N_HEADS * HEAD_DIM
KV_DIM = N_KV_HEADS * HEAD_DIM
ROPE_AXIS_DIM = HEAD_DIM // 2
ROPE_HALF = ROPE_AXIS_DIM // 2
ROPE_THETA = 10000.0
Q_BLOCK = 128

N_EVEN = (DEPTH + 1) // 2
N_ODD = DEPTH // 2

kernel_name = "hybrid_sgu_s5_gqa_prefix_dit_block"


def _rms(x, g):
    xf = x.astype(jnp.float32)
    y = xf * lax.rsqrt(jnp.mean(xf * xf, axis=-1, keepdims=True) + NORM_EPS)
    return (y * g.astype(jnp.float32)).astype(x.dtype)


def _modulate_pre(x, g, shift, scale):
    return _rms(x, g) * (1 + scale[:, None]) + shift[:, None]


def _gated_post(x, y, g, gate, weight):
    return x + weight * gate[:, None] * _rms(y, g)


def _swiglu_sub(x, mod, j, g_pre, g_post, w_in, w_out):
    h = _modulate_pre(x, g_pre, mod[:, 3 * j], mod[:, 3 * j + 1])
    gate, up = jnp.split(h @ w_in, 2, axis=-1)
    y = (jax.nn.silu(gate) * up) @ w_out
    return _gated_post(x, y, g_post, mod[:, 3 * j + 2], FFN_RES)


def _chunk_sgu(p, norm_g, w_s, b_s):
    bsz, t, _ = p.shape
    u = jax.nn.gelu(p[..., :D_A])
    v = jax.nn.gelu(p[..., D_A:]).reshape(bsz, t // CHUNK, CHUNK, A_GROUPS, A_GROUP_DIM)
    vf = v.astype(jnp.float32)
    mu = jnp.mean(vf, axis=-1, keepdims=True)
    var = jnp.mean(jnp.square(vf - mu), axis=-1, keepdims=True)
    vn = ((vf - mu) * lax.rsqrt(var + NORM_EPS)
          * norm_g.reshape(A_GROUPS, A_GROUP_DIM).astype(jnp.float32)).astype(p.dtype)
    mixed = jnp.einsum('gts,bnsgc->bntgc', w_s, vn) + b_s.T[:, :, None]
    return u * mixed.reshape(bsz, t, D_A)


def _s5_discretise(lam_re, lam_im, log_step, b_re, b_im, c_re, c_im):
    f32 = jnp.float32
    lam = lax.complex(lam_re.astype(f32), lam_im.astype(f32))
    dt = jnp.exp(log_step.astype(f32))[..., None]
    lam_bar = jnp.exp(lam * dt)
    b_mat = lax.complex(b_re.astype(f32), b_im.astype(f32))
    b_bar = ((lam_bar - 1.0) / lam)[..., None] * b_mat
    c_mat = lax.complex(c_re.astype(f32), c_im.astype(f32))
    return lam_bar, b_bar, c_mat


def _ssm_combine(left, right):
    a_l, h_l = left
    a_r, h_r = right
    return a_l * a_r, a_r * h_l + h_r


def _s5_states(u, lam_bar, b_bar, h0, reverse):
    bu = jnp.einsum('btgc,gpc->btgp', u.astype(jnp.complex64), b_bar)
    if h0 is not None:
        edge = -1 if reverse else 0
        bu = bu.at[:, edge].add(lam_bar * h0)
    a = jnp.broadcast_to(lam_bar, (1,) + bu.shape[1:])
    _, h = lax.associative_scan(_ssm_combine, (a, bu), reverse=reverse, axis=1)
    return h


def _s5_readout(u, h_f, h_b, c_mat, d_skip, glu_w, glu_b, dtype):
    y = (jnp.real(jnp.einsum('btgp,gcp->btgc', h_f, c_mat[0]))
         + jnp.real(jnp.einsum('btgp,gcp->btgc', h_b, c_mat[1])))
    y = y.reshape(*y.shape[:2], D_B) + d_skip.astype(jnp.float32) * u.reshape(*u.shape[:2], D_B)
    y = jax.nn.gelu(y).astype(dtype)
    return y * jax.nn.sigmoid(y @ glu_w + glu_b)


def _mixer_sgu_s5(hl, hc, need_ctx, w_in, w_out, sgu_g, sgu_w, sgu_b, lam_re, lam_im, log_step,
                  b_re, b_im, c_re, c_im, d_skip, glu_w, glu_b):
    lam_bar, b_bar, c_mat = _s5_discretise(lam_re, lam_im, log_step, b_re, b_im, c_re, c_im)
    pl = hl @ w_in
    pc = hc @ w_in

    def s5_in(p):
        return p[..., 2 * D_A:].astype(jnp.float32).reshape(*p.shape[:2], S5_GROUPS, S5_GROUP)

    uc, ul = s5_in(pc), s5_in(pl)
    hc_f = _s5_states(uc, lam_bar[0], b_bar[0], None, False)
    hc_b = _s5_states(uc, lam_bar[1], b_bar[1], None, True)
    hl_f = _s5_states(ul, lam_bar[0], b_bar[0], hc_f[:, -1], False)
    hl_b = _s5_states(ul, lam_bar[1], b_bar[1], hc_b[:, 0], True)

    def merge(p, u, h_f, h_b):
        ya = _chunk_sgu(p[..., :2 * D_A], sgu_g, sgu_w, sgu_b)
        yb = _s5_readout(u, h_f, h_b, c_mat, d_skip, glu_w, glu_b, p.dtype)
        return jnp.concatenate([ya, yb], axis=-1) @ w_out

    yl = merge(pl, ul, hl_f, hl_b)
    yc = merge(pc, uc, hc_f, hc_b) if need_ctx else None
    return yl, yc


def _rope_tables(rows):
    f32 = jnp.float32
    row_id = jnp.repeat(jnp.arange(rows, dtype=f32), GRID_W)
    col_id = jnp.tile(jnp.arange(GRID_W, dtype=f32), rows)
    inv_freq = ROPE_THETA ** (-jnp.arange(0, ROPE_AXIS_DIM, 2, dtype=f32) / ROPE_AXIS_DIM)
    ang = jnp.stack([row_id[:, None] * inv_freq, col_id[:, None] * inv_freq], axis=1)
    return jnp.cos(ang), jnp.sin(ang)


def _rope2d(x, cos, sin):
    xf = x.astype(jnp.float32).reshape(*x.shape[:-1], 2, 2, ROPE_HALF)
    x1, x2 = xf[..., 0, :], xf[..., 1, :]
    c = cos[None, :, None]
    s = sin[None, :, None]
    out = jnp.stack([x1 * c - x2 * s, x2 * c + x1 * s], axis=-2)
    return out.reshape(x.shape).astype(x.dtype)


def _heads(p, n):
    return p.reshape(*p.shape[:2], n, HEAD_DIM)


def _attend(q, k, v):
    bsz, t = q.shape[:2]
    nb = t // Q_BLOCK
    qb = q.reshape(bsz, nb, Q_BLOCK, N_KV_HEADS, Q_PER_KV, HEAD_DIM).transpose(1, 0, 2, 3, 4, 5)
    scale = HEAD_DIM ** -0.5

    def one_block(qi):
        s = jnp.einsum('bqkgd,blkd->bkgql', qi, k).astype(jnp.float32) * scale
        pr = jax.nn.softmax(s, axis=-1).astype(v.dtype)
        return jnp.einsum('bkgql,blkd->bqkgd', pr, v)

    o = lax.map(one_block, qb)
    return o.transpose(1, 0, 2, 3, 4, 5).reshape(bsz, t, Q_DIM)


def _mixer_gqa(hl, hc, need_ctx, w_qkv, w_out, q_g, k_g, cos, sin):
    pkv_c = hc @ w_qkv[:, Q_DIM:]
    kc = _rms(_heads(pkv_c[..., :KV_DIM], N_KV_HEADS), k_g)
    vc = _heads(pkv_c[..., KV_DIM:], N_KV_HEADS)
    pl = hl @ w_qkv
    ql = _rope2d(_rms(_heads(pl[..., :Q_DIM], N_HEADS), q_g), cos, sin)
    kl = _rope2d(_rms(_heads(pl[..., Q_DIM:Q_DIM + KV_DIM], N_KV_HEADS), k_g), cos, sin)
    vl = _heads(pl[..., Q_DIM + KV_DIM:], N_KV_HEADS)
    yl = _attend(ql, jnp.concatenate([kc, kl], axis=1), jnp.concatenate([vc, vl], axis=1)) @ w_out
    yc = None
    if need_ctx:
        qc = _rms(_heads(hc @ w_qkv[:, :Q_DIM], N_HEADS), q_g)
        yc = _attend(qc, kc, vc) @ w_out
    return yl, yc


def setup_inputs(seed: int = 0) -> dict:
    key = jax.random.key(seed)
    ks = iter(jax.random.split(key, 40))
    f32 = jnp.float32

    def nrm(shape, scale):
        return scale * jax.random.normal(next(ks), shape, f32)

    x = nrm((BATCH, SEQ, D_MODEL), 1.0)
    c = nrm((BATCH, D_MODEL), 1.0)
    ctx = nrm((BATCH, CTX_LEN, D_MODEL), 1.0)
    c_ctx = nrm((D_MODEL,), 1.0)
    w_mod = nrm((DEPTH, D_MODEL, N_MOD * D_MODEL), 0.5 * D_MODEL ** -0.5)
    b_mod = nrm((DEPTH, N_MOD * D_MODEL), 0.02)
    norm_pre = 1.0 + nrm((DEPTH, N_SUB, D_MODEL), 0.02)
    norm_post = 1.0 + nrm((DEPTH, N_SUB, D_MODEL), 0.02)
    ffn_w_in = nrm((DEPTH, 2, D_MODEL, 2 * D_FF), D_MODEL ** -0.5)
    ffn_w_out = nrm((DEPTH, 2, D_FF, D_MODEL), D_FF ** -0.5)
    ab_w_in = nrm((N_EVEN, D_MODEL, 2 * D_A + D_B), D_MODEL ** -0.5)
    ab_w_out = nrm((N_EVEN, D_A + D_B, D_MODEL), (D_A + D_B) ** -0.5)
    sgu_norm_g = 1.0 + nrm((N_EVEN, D_A), 0.02)
    sgu_w = nrm((N_EVEN, A_GROUPS, CHUNK, CHUNK), CHUNK ** -0.5)
    sgu_b = 1.0 + nrm((N_EVEN, A_GROUPS, CHUNK), 0.02)
    s5_lam_re = -0.5 * jnp.exp(nrm((N_EVEN, N_DIR, S5_GROUPS, S5_STATE), 0.05))
    s5_lam_im = (math.pi * jnp.arange(S5_STATE, dtype=f32)
                 + nrm((N_EVEN, N_DIR, S5_GROUPS, S5_STATE), 0.01))
    s5_log_step = jax.random.uniform(next(ks), (N_EVEN, N_DIR, S5_GROUPS), f32,
                                     minval=math.log(1e-3), maxval=math.log(1e-1))
    s5_b_re = nrm((N_EVEN, N_DIR, S5_GROUPS, S5_STATE, S5_GROUP), (2 * S5_GROUP) ** -0.5)
    s5_b_im = nrm((N_EVEN, N_DIR, S5_GROUPS, S5_STATE, S5_GROUP), (2 * S5_GROUP) ** -0.5)
    s5_c_re = nrm((N_EVEN, N_DIR, S5_GROUPS, S5_GROUP, S5_STATE), (2 * S5_STATE) ** -0.5)
    s5_c_im = nrm((N_EVEN, N_DIR, S5_GROUPS, S5_GROUP, S5_STATE), (2 * S5_STATE) ** -0.5)
    s5_d = nrm((N_EVEN, D_B), 1.0)
    s5_glu_w = nrm((N_EVEN, D_B, D_B), D_B ** -0.5)
    s5_glu_b = nrm((N_EVEN, D_B), 0.02)
    attn_w_qkv = nrm((N_ODD, D_MODEL, Q_DIM + 2 * KV_DIM), D_MODEL ** -0.5)
    attn_w_out = nrm((N_ODD, Q_DIM, D_MODEL), Q_DIM ** -0.5)
    attn_q_norm = 1.0 + nrm((N_ODD, HEAD_DIM), 0.02)
    attn_k_norm = 1.0 + nrm((N_ODD, HEAD_DIM), 0.02)
    return {"x": x, "c": c, "ctx": ctx, "c_ctx": c_ctx, "w_mod": w_mod, "b_mod": b_mod,
            "norm_pre": norm_pre, "norm_post": norm_post, "ffn_w_in": ffn_w_in, "ffn_w_out": ffn_w_out,
            "ab_w_in": ab_w_in, "ab_w_out": ab_w_out, "sgu_norm_g": sgu_norm_g, "sgu_w": sgu_w,
            "sgu_b": sgu_b, "s5_lam_re": s5_lam_re, "s5_lam_im": s5_lam_im, "s5_log_step": s5_log_step,
            "s5_b_re": s5_b_re, "s5_b_im": s5_b_im, "s5_c_re": s5_c_re, "s5_c_im": s5_c_im,
            "s5_d": s5_d, "s5_glu_w": s5_glu_w, "s5_glu_b": s5_glu_b, "attn_w_qkv": attn_w_qkv,
            "attn_w_out": attn_w_out, "attn_q_norm": attn_q_norm, "attn_k_norm": attn_k_norm}


def reference(x, c, ctx, c_ctx, w_mod, b_mod, norm_pre, norm_post, ffn_w_in, ffn_w_out,
              ab_w_in, ab_w_out, sgu_norm_g, sgu_w, sgu_b, s5_lam_re, s5_lam_im, s5_log_step,
              s5_b_re, s5_b_im, s5_c_re, s5_c_im, s5_d, s5_glu_w, s5_glu_b,
              attn_w_qkv, attn_w_out, attn_q_norm, attn_k_norm):
    rows = x.shape[1] // GRID_W
    cos, sin = _rope_tables(rows)
    cond_l = jax.nn.silu(c)
    cond_c = jax.nn.silu(c_ctx)[None]
    xl, xc = x, ctx
    for i in range(DEPTH):
        last = i == DEPTH - 1
        j = i // 2
        mod_l = (cond_l @ w_mod[i] + b_mod[i]).reshape(-1, N_MOD, D_MODEL)
        mod_c = (cond_c @ w_mod[i] + b_mod[i]).reshape(-1, N_MOD, D_MODEL)
        ffn1 = (norm_pre[i, 0], norm_post[i, 0], ffn_w_in[i, 0], ffn_w_out[i, 0])
        ffn2 = (norm_pre[i, 2], norm_post[i, 2], ffn_w_in[i, 1], ffn_w_out[i, 1])
        xl = _swiglu_sub(xl, mod_l, 0, *ffn1)
        xc = _swiglu_sub(xc, mod_c, 0, *ffn1)
        hl = _modulate_pre(xl, norm_pre[i, 1], mod_l[:, 3], mod_l[:, 4])
        hc = _modulate_pre(xc, norm_pre[i, 1], mod_c[:, 3], mod_c[:, 4])
        if i % 2 == 0:
            yl, yc = _mixer_sgu_s5(hl, hc, not last, ab_w_in[j], ab_w_out[j], sgu_norm_g[j], sgu_w[j],
                                   sgu_b[j], s5_lam_re[j], s5_lam_im[j], s5_log_step[j], s5_b_re[j],
                                   s5_b_im[j], s5_c_re[j], s5_c_im[j], s5_d[j], s5_glu_w[j], s5_glu_b[j])
        else:
            yl, yc = _mixer_gqa(hl, hc, not last, attn_w_qkv[j], attn_w_out[j], attn_q_norm[j],
                                attn_k_norm[j], cos, sin)
        xl = _gated_post(xl, yl, norm_post[i, 1], mod_l[:, 5], 1.0)
        xl = _swiglu_sub(xl, mod_l, 2, *ffn2)
        if not last:
            xc = _gated_post(xc, yc, norm_post[i, 1], mod_c[:, 5], 1.0)
            xc = _swiglu_sub(xc, mod_c, 2, *ffn2)
    return xl
```

```python
import functools
import math

import jax
import jax.numpy as jnp
from jax import lax
from jax.experimental import pallas as pl
from jax.experimental.pallas import tpu as pltpu

F32 = jnp.float32
BF16 = jnp.bfloat16

NORM_EPS = 1e-6
FFN_RES = 0.5
GRID_W = 64
ROPE_THETA = 10000.0
SGU_CHUNK = 128
HEAD_DIM = 128

V7X_LANES = 128
V7X_SUBLANES = 8
V7X_VMEM_BYTES = 64 * 1024 * 1024
VMEM_LIMIT = 56 * 1024 * 1024

N_MOD_ROWS = 16
CTX_MOD_ROW = 8


def _tiles():
    return dict(ffn=512, mix=512, attn_q=512, s5_steps=32, mod_n=2304)


def _params(sem, vmem=VMEM_LIMIT):
    return pltpu.CompilerParams(dimension_semantics=sem, vmem_limit_bytes=vmem)


def _const_spec(shape):
    nd = len(shape)
    return pl.BlockSpec(shape, lambda *_: (0,) * nd)


def _rms(x):
    return x * lax.rsqrt(jnp.mean(x * x, axis=-1, keepdims=True) + NORM_EPS)


def _mm(a, b):
    return jnp.dot(a, b, preferred_element_type=F32)


def _mod_kernel(c_ref, w_ref, b_ref, o_ref):
    c = c_ref[...]
    s = (c * jax.nn.sigmoid(c)).astype(BF16)
    o_ref[...] = _mm(s, w_ref[...].astype(BF16)) + b_ref[...]


def _modulation(cond, w_mod, b_mod):
    depth, d, n = w_mod.shape
    tn = _tiles()["mod_n"]
    return pl.pallas_call(
        _mod_kernel,
        grid=(depth, n // tn),
        in_specs=[_const_spec((N_MOD_ROWS, d)),
                  pl.BlockSpec((None, d, tn), lambda l, j: (l, 0, j)),
                  pl.BlockSpec((None, 1, tn), lambda l, j: (l, 0, j))],
        out_specs=pl.BlockSpec((None, N_MOD_ROWS, tn), lambda l, j: (l, 0, j)),
        out_shape=jax.ShapeDtypeStruct((depth, N_MOD_ROWS, n), F32),
        compiler_params=_params(("parallel", "parallel")),
        name="modulation",
    )(cond, w_mod, b_mod.reshape(depth, 1, n))


def _ffn_chunks(d_ff, width=512):
    return [(c, min(c + width, d_ff)) for c in range(0, d_ff, width)]


def _ffn_kernel(sub, x_ref, mod_ref, g_ref, win_ref, wout_ref, o_ref, a_scr):
    d_ff = wout_ref.shape[0]
    x = x_ref[...]
    shift, scale, gate = (mod_ref[3 * sub + k:3 * sub + k + 1, :] for k in range(3))
    g_pre, g_post = g_ref[0:1, :], g_ref[1:2, :]
    h = (_rms(x) * g_pre * (1.0 + scale) + shift).astype(BF16)
    for c0, c1 in _ffn_chunks(d_ff):
        g = _mm(h, win_ref[:, c0:c1])
        u = _mm(h, win_ref[:, d_ff + c0:d_ff + c1])
        a_scr[:, c0:c1] = (g * jax.nn.sigmoid(g) * u).astype(BF16)
    y = _mm(a_scr[...], wout_ref[...])
    o_ref[...] = x + (FFN_RES * gate) * (_rms(y) * g_post)


def _ffn(x, mod, gains, w_in, w_out, sub, row_of):
    r, d = x.shape
    d_ff = w_out.shape[0]
    tm = _tiles()["ffn"]
    return pl.pallas_call(
        functools.partial(_ffn_kernel, sub),
        grid=(r // tm,),
        in_specs=[pl.BlockSpec((tm, d), lambda i: (i, 0)),
                  pl.BlockSpec((None, N_MOD_ROWS, d), lambda i: (row_of(i, tm), 0, 0)),
                  _const_spec(gains.shape),
                  _const_spec(w_in.shape),
                  _const_spec(w_out.shape)],
        out_specs=pl.BlockSpec((tm, d), lambda i: (i, 0)),
        out_shape=jax.ShapeDtypeStruct((r, d), F32),
        scratch_shapes=[pltpu.VMEM((tm, d_ff), BF16)],
        compiler_params=_params(("parallel",)),
        name=f"ffn{sub}",
    )(x, mod, gains, w_in, w_out)


def _mix0_in_kernel(x_ref, mod_ref, g_ref, w_ref, ng_ref, ug_ref, vn_ref, us_ref):
    d_a = ug_ref.shape[1]
    x = x_ref[...]
    shift, scale = mod_ref[3:4, :], mod_ref[4:5, :]
    h = (_rms(x) * g_ref[0:1, :] * (1.0 + scale) + shift).astype(BF16)
    ug_ref[...] = jax.nn.gelu(_mm(h, w_ref[:, 0:d_a])).astype(BF16)
    for g in range(d_a // SGU_CHUNK):
        c0 = g * SGU_CHUNK
        v = jax.nn.gelu(_mm(h, w_ref[:, d_a + c0:d_a + c0 + SGU_CHUNK]))
        mu = jnp.mean(v, axis=-1, keepdims=True)
        vc = v - mu
        var = jnp.mean(vc * vc, axis=-1, keepdims=True)
        vn_ref[:, c0:c0 + SGU_CHUNK] = (vc * lax.rsqrt(var + NORM_EPS)
                                        * ng_ref[:, c0:c0 + SGU_CHUNK]).astype(BF16)
    us_ref[...] = _mm(h, w_ref[:, 2 * d_a:])


def _mix0_in(x, mod, gains, w_in, norm_g, d_a, row_of):
    r, d = x.shape
    d_b = w_in.shape[1] - 2 * d_a
    tm = _tiles()["mix"]
    rows = lambda w: pl.BlockSpec((tm, w), lambda i: (i, 0))
    return pl.pallas_call(
        _mix0_in_kernel,
        grid=(r // tm,),
        in_specs=[rows(d),
                  pl.BlockSpec((None, N_MOD_ROWS, d), lambda i: (row_of(i, tm), 0, 0)),
                  _const_spec(gains.shape), _const_spec(w_in.shape), _const_spec(norm_g.shape)],
        out_specs=[rows(d_a), rows(d_a), rows(d_b)],
        out_shape=[jax.ShapeDtypeStruct((r, d_a), BF16), jax.ShapeDtypeStruct((r, d_a), BF16),
                   jax.ShapeDtypeStruct((r, d_b), F32)],
        compiler_params=_params(("parallel",)),
        name="mix0_in",
    )(x, mod, gains, w_in, norm_g)


def _s5_kernel(steps, uf_ref, ub_ref, bf_ref, bb_ref, cf_ref, cb_ref, lam_ref,
               yf_ref, yb_ref, sf_scr, sb_scr, hf_scr, hb_scr):
    nb = V7X_SUBLANES
    n = lam_ref.shape[1]

    @pl.when(pl.program_id(0) == 0)
    def _():
        hf_scr[...] = jnp.zeros_like(hf_scr)
        hb_scr[...] = jnp.zeros_like(hb_scr)

    sf_scr[...] = _mm(uf_ref[...].astype(BF16), bf_ref[...])
    sb_scr[...] = _mm(ub_ref[...].astype(BF16), bb_ref[...])

    def scan(s_scr, h_scr, lam_row, reverse):
        lr = lam_ref[lam_row:lam_row + nb, :]
        li = lam_ref[lam_row + nb:lam_row + 2 * nb, :]

        def step(k, carry):
            hr, hi = carry
            t = (steps - 1 - k) if reverse else k
            rows = pl.ds(pl.multiple_of(t * nb, nb), nb)
            nr = lr * hr - li * hi + s_scr[rows, 0:n]
            ni = lr * hi + li * hr + s_scr[rows, n:2 * n]
            s_scr[rows, 0:n] = nr
            s_scr[rows, n:2 * n] = ni
            return nr, ni

        hr, hi = lax.fori_loop(0, steps, step, (h_scr[:, 0:n], h_scr[:, n:2 * n]))
        h_scr[:, 0:n] = hr
        h_scr[:, n:2 * n] = hi

    scan(sf_scr, hf_scr, 0, False)
    scan(sb_scr, hb_scr, 2 * nb, True)
    yf_ref[...] = _mm(sf_scr[...].astype(BF16), cf_ref[...])
    yb_ref[...] = _mm(sb_scr[...].astype(BF16), cb_ref[...])


def _s5_scan(u_tm, n_ctx_steps, b_mats, c_mats, lam):
    rows, d_b = u_tm.shape
    nb = V7X_SUBLANES
    steps = _tiles()["s5_steps"]
    n_chunks = rows // (steps * nb)
    n_ctx = n_ctx_steps // steps
    n2 = b_mats.shape[-1]

    def bwd_chunk(k):
        return jnp.where(k < n_ctx, n_ctx - 1 - k, n_chunks - 1 - (k - n_ctx))

    blk = (steps * nb, d_b)
    return pl.pallas_call(
        functools.partial(_s5_kernel, steps),
        grid=(n_chunks,),
        in_specs=[pl.BlockSpec(blk, lambda k: (k, 0)),
                  pl.BlockSpec(blk, lambda k: (bwd_chunk(k), 0)),
                  pl.BlockSpec((None, d_b, n2), lambda k: (0, 0, 0)),
                  pl.BlockSpec((None, d_b, n2), lambda k: (1, 0, 0)),
                  pl.BlockSpec((None, n2, d_b), lambda k: (0, 0, 0)),
                  pl.BlockSpec((None, n2, d_b), lambda k: (1, 0, 0)),
                  _const_spec(lam.shape)],
        out_specs=[pl.BlockSpec(blk, lambda k: (k, 0)),
                   pl.BlockSpec(blk, lambda k: (bwd_chunk(k), 0))],
        out_shape=[jax.ShapeDtypeStruct((rows, d_b), F32)] * 2,
        scratch_shapes=[pltpu.VMEM((steps * nb, n2), F32), pltpu.VMEM((steps * nb, n2), F32),
                        pltpu.VMEM((nb, n2), F32), pltpu.VMEM((nb, n2), F32)],
        compiler_params=_params(("arbitrary",)),
        name="s5_scan",
    )(u_tm, u_tm, b_mats, b_mats, c_mats, c_mats, lam)


def _s5_operands(lam_re, lam_im, log_step, b_re, b_im, c_re, c_im):
    n_dir, groups, states = lam_re.shape
    gdim = b_re.shape[-1]
    dt = jnp.exp(log_step.astype(F32))[..., None]
    lr, li = lam_re.astype(F32), lam_im.astype(F32)
    mag = jnp.exp(lr * dt)
    ar, ai = mag * jnp.cos(li * dt), mag * jnp.sin(li * dt)
    den = lr * lr + li * li
    fr = ((ar - 1.0) * lr + ai * li) / den
    fi = (ai * lr - (ar - 1.0) * li) / den
    bbr = fr[..., None] * b_re - fi[..., None] * b_im
    bbi = fr[..., None] * b_im + fi[..., None] * b_re
    eye = jnp.eye(groups, dtype=F32)
    n = groups * states

    def in_mat(b):
        return jnp.einsum("dgpc,gh->dgchp", b, eye).reshape(n_dir, groups * gdim, n)

    def out_mat(c):
        return jnp.einsum("dgcp,gh->dgphc", c, eye).reshape(n_dir, n, groups * gdim)

    b_mats = jnp.concatenate([in_mat(bbr), in_mat(bbi)], axis=-1).astype(BF16)
    c_mats = jnp.concatenate([out_mat(c_re.astype(F32)), -out_mat(c_im.astype(F32))], axis=1).astype(BF16)
    nb = V7X_SUBLANES
    lam = jnp.concatenate([jnp.broadcast_to(v.reshape(1, n), (nb, n))
                           for v in (ar[0], ai[0], ar[1], ai[1])], axis=0)
    return b_mats, c_mats, lam


def _mix0_out_kernel(x_ref, mod_ref, g_ref, ug_ref, vn_ref, yf_ref, yb_ref, us_ref,
                     sw_ref, sb_ref, d_ref, gw_ref, gb_ref, wo_ref, o_ref, z_scr):
    tm, d_a = ug_ref.shape
    for n in range(tm // SGU_CHUNK):
        r0 = n * SGU_CHUNK
        for g in range(d_a // SGU_CHUNK):
            c0 = g * SGU_CHUNK
            mixed = _mm(sw_ref[g], vn_ref[r0:r0 + SGU_CHUNK, c0:c0 + SGU_CHUNK]) + sb_ref[g]
            z_scr[r0:r0 + SGU_CHUNK, c0:c0 + SGU_CHUNK] = (
                ug_ref[r0:r0 + SGU_CHUNK, c0:c0 + SGU_CHUNK].astype(F32) * mixed).astype(BF16)
    ys = jax.nn.gelu(yf_ref[...] + yb_ref[...] + d_ref[...] * us_ref[...])
    gl = jax.nn.sigmoid(_mm(ys.astype(BF16), gw_ref[...]) + gb_ref[...])
    z_scr[:, d_a:] = (ys * gl).astype(BF16)
    y = _mm(z_scr[...], wo_ref[...])
    o_ref[...] = x_ref[...] + mod_ref[5:6, :] * (_rms(y) * g_ref[1:2, :])


def _mix0_out(x, mod, gains, ug, vn, yf, yb, us, sgu_w, sgu_bias, d_skip, glu_w, glu_b, w_out, row_of):
    r, d = x.shape
    tm = _tiles()["mix"]
    rows = lambda a: pl.BlockSpec((tm, a.shape[1]), lambda i: (i, 0))
    consts = (sgu_w, sgu_bias, d_skip, glu_w, glu_b, w_out)
    return pl.pallas_call(
        _mix0_out_kernel,
        grid=(r // tm,),
        in_specs=[rows(x),
                  pl.BlockSpec((None, N_MOD_ROWS, d), lambda i: (row_of(i, tm), 0, 0)),
                  _const_spec(gains.shape), rows(ug), rows(vn), rows(yf), rows(yb), rows(us)]
                 + [_const_spec(c.shape) for c in consts],
        out_specs=rows(x),
        out_shape=jax.ShapeDtypeStruct((r, d), F32),
        scratch_shapes=[pltpu.VMEM((tm, d), BF16)],
        compiler_params=_params(("parallel",)),
        name="mix0_out",
    )(x, mod, gains, ug, vn, yf, yb, us, *consts)


def _rope(x, cos, sin_lo, sin_hi):
    quarter = HEAD_DIM // 4
    return (x * cos + pltpu.roll(x, HEAD_DIM - quarter, axis=1) * sin_lo
            + pltpu.roll(x, quarter, axis=1) * sin_hi)


def _qkv_kernel(with_q, x_ref, mod_ref, g_ref, w_ref, ng_ref, rope_ref, *out_refs):
    if with_q:
        q_ref, k_ref, v_ref = out_refs
    else:
        k_ref, v_ref = out_refs
    q_dim = w_ref.shape[1] - k_ref.shape[1] - v_ref.shape[1]
    kv_dim = k_ref.shape[1]
    x = x_ref[...]
    shift, scale = mod_ref[3:4, :], mod_ref[4:5, :]
    h = (_rms(x) * g_ref[0:1, :] * (1.0 + scale) + shift).astype(BF16)
    if with_q:
        cos, sin_lo, sin_hi = rope_ref[0], rope_ref[1], rope_ref[2]
        q_gain = ng_ref[0:1, :] * (HEAD_DIM ** -0.5)
        for hd in range(q_dim // HEAD_DIM):
            c0 = hd * HEAD_DIM
            q = _rms(_mm(h, w_ref[:, c0:c0 + HEAD_DIM])) * q_gain
            q_ref[:, c0:c0 + HEAD_DIM] = _rope(q, cos, sin_lo, sin_hi).astype(BF16)
    for hd in range(kv_dim // HEAD_DIM):
        c0 = hd * HEAD_DIM
        k = _rms(_mm(h, w_ref[:, q_dim + c0:q_dim + c0 + HEAD_DIM])) * ng_ref[1:2, :]
        if with_q:
            k = _rope(k, cos, sin_lo, sin_hi)
        k_ref[:, c0:c0 + HEAD_DIM] = k.astype(BF16)
    v_ref[...] = _mm(h, w_ref[:, q_dim + kv_dim:]).astype(BF16)


def _qkv(x, mod, gains, w_qkv, qk_gains, rope_tab, kv_dim, with_q, row_of, seq):
    r, d = x.shape
    q_dim = w_qkv.shape[1] - 2 * kv_dim
    tm = _tiles()["mix"]
    rows = lambda w: pl.BlockSpec((tm, w), lambda i: (i, 0))
    pos_blocks = seq // tm
    out_specs = [rows(kv_dim), rows(kv_dim)]
    out_shape = [jax.ShapeDtypeStruct((r, kv_dim), BF16)] * 2
    if with_q:
        out_specs = [rows(q_dim)] + out_specs
        out_shape = [jax.ShapeDtypeStruct((r, q_dim), BF16)] + out_shape
    return pl.pallas_call(
        functools.partial(_qkv_kernel, with_q),
        grid=(r // tm,),
        in_specs=[rows(d),
                  pl.BlockSpec((None, N_MOD_ROWS, d), lambda i: (row_of(i, tm), 0, 0)),
                  _const_spec(gains.shape), _const_spec(w_qkv.shape), _const_spec(qk_gains.shape),
                  pl.BlockSpec((3, tm, HEAD_DIM), lambda i: (0, i % pos_blocks, 0))],
        out_specs=out_specs,
        out_shape=out_shape,
        compiler_params=_params(("parallel",)),
        name="qkv" if with_q else "kv_ctx",
    )(x, mod, gains, w_qkv, qk_gains, rope_tab)


def _rope_tables(seq):
    rows = seq // GRID_W
    axis_dim = HEAD_DIM // 2
    quarter = axis_dim // 2
    row_id = jnp.repeat(jnp.arange(rows, dtype=F32), GRID_W)
    col_id = jnp.tile(jnp.arange(GRID_W, dtype=F32), rows)
    inv_freq = ROPE_THETA ** (-jnp.arange(0, axis_dim, 2, dtype=F32) / axis_dim)
    a_row, a_col = row_id[:, None] * inv_freq, col_id[:, None] * inv_freq
    zero = jnp.zeros((seq, quarter), F32)
    cos = jnp.concatenate([jnp.cos(a_row)] * 2 + [jnp.cos(a_col)] * 2, axis=1)
    sin_lo = jnp.concatenate([-jnp.sin(a_row), zero, -jnp.sin(a_col), zero], axis=1)
    sin_hi = jnp.concatenate([zero, jnp.sin(a_row), zero, jnp.sin(a_col)], axis=1)
    return jnp.stack([cos, sin_lo, sin_hi])


def _attn_kernel(q_ref, k_ref, v_ref, o_ref):
    s = lax.dot_general(q_ref[...], k_ref[...], (((1,), (1,)), ((), ())), preferred_element_type=F32)
    p = jnp.exp(s - jnp.max(s, axis=-1, keepdims=True))
    l = jnp.sum(p, axis=-1, keepdims=True)
    o_ref[...] = (_mm(p.astype(BF16), v_ref[...]) / l).astype(BF16)


def _attention(q, k, v, q_per_kv):
    bsz, t, q_dim = q.shape
    l = k.shape[1]
    kvh = k.shape[2] // HEAD_DIM
    tq = _tiles()["attn_q"]
    q_spec = pl.BlockSpec((None, tq, HEAD_DIM), lambda b, j, i, h: (b, i, j * q_per_kv + h))
    kv_spec = pl.BlockSpec((None, l, HEAD_DIM), lambda b, j, i, h: (b, 0, j))
    return pl.pallas_call(
        _attn_kernel,
        grid=(bsz, kvh, t // tq, q_per_kv),
        in_specs=[q_spec, kv_spec, kv_spec],
        out_specs=q_spec,
        out_shape=jax.ShapeDtypeStruct((bsz, t, q_dim), BF16),
        compiler_params=_params(("parallel",) * 4),
        name="attention",
    )(q, k, v)


def _attn_out_kernel(x_ref, mod_ref, g_ref, a_ref, wo_ref, o_ref):
    y = _mm(a_ref[...], wo_ref[...])
    o_ref[...] = x_ref[...] + mod_ref[5:6, :] * (_rms(y) * g_ref[1:2, :])


def _attn_out(x, mod, gains, a, w_out, row_of):
    r, d = x.shape
    tm = _tiles()["mix"]
    rows = lambda w: pl.BlockSpec((tm, w), lambda i: (i, 0))
    return pl.pallas_call(
        _attn_out_kernel,
        grid=(r // tm,),
        in_specs=[rows(d),
                  pl.BlockSpec((None, N_MOD_ROWS, d), lambda i: (row_of(i, tm), 0, 0)),
                  _const_spec(gains.shape), rows(a.shape[1]), _const_spec(w_out.shape)],
        out_specs=rows(d),
        out_shape=jax.ShapeDtypeStruct((r, d), F32),
        compiler_params=_params(("parallel",)),
        name="attn_out",
    )(x, mod, gains, a, w_out)


def kernel(x, c, ctx, c_ctx, w_mod, b_mod, norm_pre, norm_post, ffn_w_in, ffn_w_out, ab_w_in, ab_w_out, sgu_norm_g, sgu_w, sgu_b, s5_lam_re, s5_lam_im, s5_log_step, s5_b_re, s5_b_im, s5_c_re, s5_c_im, s5_d, s5_glu_w, s5_glu_b, attn_w_qkv, attn_w_out, attn_q_norm, attn_k_norm):
    bsz, seq, d = x.shape
    n_ctx = ctx.shape[1]
    depth = w_mod.shape[0]
    assert bsz == V7X_SUBLANES, "the S5 scan keeps the batch on the sublane axis"
    d_a = sgu_norm_g.shape[1]
    d_b = s5_d.shape[1]
    kv_dim = (attn_w_qkv.shape[2] - d) // 2
    q_per_kv = d // kv_dim

    def lat_row(i, tm):
        return (i * tm) // seq

    def ctx_row(i, tm):
        return CTX_MOD_ROW

    cond = jnp.zeros((N_MOD_ROWS, d), F32).at[:bsz].set(c).at[CTX_MOD_ROW].set(c_ctx)
    mod_all = _modulation(cond, w_mod, b_mod).reshape(depth, N_MOD_ROWS, -1, d)
    mod_all = jnp.pad(mod_all, ((0, 0), (0, 0), (0, N_MOD_ROWS - mod_all.shape[2]), (0, 0)))

    xl = x.reshape(bsz * seq, d)
    xc = ctx.reshape(bsz * n_ctx, d)
    streams = ((lat_row, seq), (ctx_row, n_ctx))

    for i in range(depth):
        last = i == depth - 1
        j = i // 2
        mod = mod_all[i]
        gains = [jnp.zeros((V7X_SUBLANES, d), F32).at[0].set(norm_pre[i, s]).at[1].set(norm_post[i, s])
                 for s in range(3)]
        w_in1, w_in2 = ffn_w_in[i, 0].astype(BF16), ffn_w_in[i, 1].astype(BF16)
        w_out1, w_out2 = ffn_w_out[i, 0].astype(BF16), ffn_w_out[i, 1].astype(BF16)

        xl = _ffn(xl, mod, gains[0], w_in1, w_out1, 0, lat_row)
        xc = _ffn(xc, mod, gains[0], w_in1, w_out1, 0, ctx_row)

        if i % 2 == 0:
            w_in = ab_w_in[j].astype(BF16)
            norm_g = sgu_norm_g[j].reshape(1, d_a)
            ug_l, vn_l, us_l = _mix0_in(xl, mod, gains[1], w_in, norm_g, d_a, lat_row)
            ug_c, vn_c, us_c = _mix0_in(xc, mod, gains[1], w_in, norm_g, d_a, ctx_row)
            u_tm = jnp.concatenate([us_c.reshape(bsz, n_ctx, d_b), us_l.reshape(bsz, seq, d_b)], axis=1)
            u_tm = u_tm.transpose(1, 0, 2).reshape((n_ctx + seq) * bsz, d_b)
            b_mats, c_mats, lam = _s5_operands(s5_lam_re[j], s5_lam_im[j], s5_log_step[j], s5_b_re[j],
                                               s5_b_im[j], s5_c_re[j], s5_c_im[j])
            yf, yb = _s5_scan(u_tm, n_ctx, b_mats, c_mats, lam)

            def batch_major(y):
                y = y.reshape(n_ctx + seq, bsz, d_b).transpose(1, 0, 2)
                return y[:, n_ctx:].reshape(bsz * seq, d_b), y[:, :n_ctx].reshape(bsz * n_ctx, d_b)

            (yf_l, yf_c), (yb_l, yb_c) = batch_major(yf), batch_major(yb)
            consts = (sgu_w[j].astype(BF16),
                      jnp.broadcast_to(sgu_b[j][:, :, None], sgu_w[j].shape).astype(F32),
                      s5_d[j].reshape(1, d_b), s5_glu_w[j].astype(BF16), s5_glu_b[j].reshape(1, d_b),
                      ab_w_out[j].astype(BF16))
            xl = _mix0_out(xl, mod, gains[1], ug_l, vn_l, yf_l, yb_l, us_l, *consts, lat_row)
            if not last:
                xc = _mix0_out(xc, mod, gains[1], ug_c, vn_c, yf_c, yb_c, us_c, *consts, ctx_row)
        else:
            if not last:
                raise NotImplementedError("context stream through an attention layer")
            w_qkv = attn_w_qkv[j].astype(BF16)
            qk_gains = jnp.zeros((V7X_SUBLANES, HEAD_DIM), F32).at[0].set(attn_q_norm[j]).at[1].set(attn_k_norm[j])
            rope_tab = _rope_tables(seq)
            q, k_l, v_l = _qkv(xl, mod, gains[1], w_qkv, qk_gains, rope_tab, kv_dim, True, lat_row, seq)
            k_c, v_c = _qkv(xc, mod, gains[1], w_qkv, qk_gains, rope_tab, kv_dim, False, ctx_row, seq)
            k = jnp.concatenate([k_c.reshape(bsz, n_ctx, kv_dim), k_l.reshape(bsz, seq, kv_dim)], axis=1)
            v = jnp.concatenate([v_c.reshape(bsz, n_ctx, kv_dim), v_l.reshape(bsz, seq, kv_dim)], axis=1)
            a = _attention(q.reshape(bsz, seq, d), k, v, q_per_kv).reshape(bsz * seq, d)
            xl = _attn_out(xl, mod, gains[1], a, attn_w_out[j].astype(BF16), lat_row)

        xl = _ffn(xl, mod, gains[2], w_in2, w_out2, 2, lat_row)
        if not last:
            xc = _ffn(xc, mod, gains[2], w_in2, w_out2, 2, ctx_row)

    return xl.reshape(bsz, seq, d)
```

```python
import functools
import math

import jax
import jax.numpy as jnp
from jax import lax
from jax.experimental import pallas as pl
from jax.experimental.pallas import tpu as pltpu

F32 = jnp.float32
BF16 = jnp.bfloat16

NORM_EPS = 1e-6
FFN_RES = 0.5
GRID_W = 64
ROPE_THETA = 10000.0
SGU_CHUNK = 128
HEAD_DIM = 128
ATTN_KEY_CHUNK = 512

V7X_LANES = 128
V7X_SUBLANES = 8
V7X_VMEM_BYTES = 64 * 1024 * 1024
VMEM_LIMIT = 56 * 1024 * 1024

N_MOD_ROWS = 16
CTX_MOD_ROW = 8


def _tiles():
    return dict(ffn=512, mix=512, attn_q=512, s5_steps=32, mod_n=2304)


def _params(sem, vmem=VMEM_LIMIT):
    return pltpu.CompilerParams(dimension_semantics=sem, vmem_limit_bytes=vmem)


def _const_spec(shape):
    nd = len(shape)
    return pl.BlockSpec(shape, lambda *_: (0,) * nd)


def _rms(x):
    return x * lax.rsqrt(jnp.mean(x * x, axis=-1, keepdims=True) + NORM_EPS)


def _mm(a, b):
    return jnp.dot(a, b, preferred_element_type=F32)


def _mod_kernel(c_ref, w_ref, b_ref, o_ref):
    c = c_ref[...]
    s = (c * jax.nn.sigmoid(c)).astype(BF16)
    o_ref[...] = _mm(s, w_ref[...].astype(BF16)) + b_ref[...]


def _modulation(cond, w_mod, b_mod):
    depth, d, n = w_mod.shape
    tn = _tiles()["mod_n"]
    return pl.pallas_call(
        _mod_kernel,
        grid=(depth, n // tn),
        in_specs=[_const_spec((N_MOD_ROWS, d)),
                  pl.BlockSpec((None, d, tn), lambda l, j: (l, 0, j)),
                  pl.BlockSpec((None, 1, tn), lambda l, j: (l, 0, j))],
        out_specs=pl.BlockSpec((None, N_MOD_ROWS, tn), lambda l, j: (l, 0, j)),
        out_shape=jax.ShapeDtypeStruct((depth, N_MOD_ROWS, n), F32),
        compiler_params=_params(("parallel", "parallel")),
        name="modulation",
    )(cond, w_mod, b_mod.reshape(depth, 1, n))


def _ffn_chunks(d_ff, width=512):
    return [(c, min(c + width, d_ff)) for c in range(0, d_ff, width)]


def _ffn_kernel(sub, x_ref, mod_ref, g_ref, win_ref, wout_ref, o_ref, a_scr):
    d_ff = wout_ref.shape[0]
    x = x_ref[...]
    shift, scale, gate = (mod_ref[3 * sub + k:3 * sub + k + 1, :] for k in range(3))
    g_pre, g_post = g_ref[0:1, :], g_ref[1:2, :]
    h = (_rms(x) * g_pre * (1.0 + scale) + shift).astype(BF16)
    for c0, c1 in _ffn_chunks(d_ff):
        g = _mm(h, win_ref[:, c0:c1])
        u = _mm(h, win_ref[:, d_ff + c0:d_ff + c1])
        a_scr[:, c0:c1] = (g * jax.nn.sigmoid(g) * u).astype(BF16)
    y = _mm(a_scr[...], wout_ref[...])
    o_ref[...] = x + (FFN_RES * gate) * (_rms(y) * g_post)


def _ffn(x, mod, gains, w_in, w_out, sub, row_of):
    r, d = x.shape
    d_ff = w_out.shape[0]
    tm = _tiles()["ffn"]
    return pl.pallas_call(
        functools.partial(_ffn_kernel, sub),
        grid=(r // tm,),
        in_specs=[pl.BlockSpec((tm, d), lambda i: (i, 0)),
                  pl.BlockSpec((None, N_MOD_ROWS, d), lambda i: (row_of(i, tm), 0, 0)),
                  _const_spec(gains.shape),
                  _const_spec(w_in.shape),
                  _const_spec(w_out.shape)],
        out_specs=pl.BlockSpec((tm, d), lambda i: (i, 0)),
        out_shape=jax.ShapeDtypeStruct((r, d), F32),
        scratch_shapes=[pltpu.VMEM((tm, d_ff), BF16)],
        compiler_params=_params(("parallel",)),
        name=f"ffn{sub}",
    )(x, mod, gains, w_in, w_out)


def _mix0_in_kernel(x_ref, mod_ref, g_ref, w_ref, ng_ref, ug_ref, vn_ref, us_ref):
    d_a = ug_ref.shape[1]
    x = x_ref[...]
    shift, scale = mod_ref[3:4, :], mod_ref[4:5, :]
    h = (_rms(x) * g_ref[0:1, :] * (1.0 + scale) + shift).astype(BF16)
    ug_ref[...] = jax.nn.gelu(_mm(h, w_ref[:, 0:d_a])).astype(BF16)
    for g in range(d_a // SGU_CHUNK):
        c0 = g * SGU_CHUNK
        v = jax.nn.gelu(_mm(h, w_ref[:, d_a + c0:d_a + c0 + SGU_CHUNK]))
        mu = jnp.mean(v, axis=-1, keepdims=True)
        vc = v - mu
        var = jnp.mean(vc * vc, axis=-1, keepdims=True)
        vn_ref[:, c0:c0 + SGU_CHUNK] = (vc * lax.rsqrt(var + NORM_EPS)
                                        * ng_ref[:, c0:c0 + SGU_CHUNK]).astype(BF16)
    us_ref[...] = _mm(h, w_ref[:, 2 * d_a:])


def _mix0_in(x, mod, gains, w_in, norm_g, d_a, row_of):
    r, d = x.shape
    d_b = w_in.shape[1] - 2 * d_a
    tm = _tiles()["mix"]
    rows = lambda w: pl.BlockSpec((tm, w), lambda i: (i, 0))
    return pl.pallas_call(
        _mix0_in_kernel,
        grid=(r // tm,),
        in_specs=[rows(d),
                  pl.BlockSpec((None, N_MOD_ROWS, d), lambda i: (row_of(i, tm), 0, 0)),
                  _const_spec(gains.shape), _const_spec(w_in.shape), _const_spec(norm_g.shape)],
        out_specs=[rows(d_a), rows(d_a), rows(d_b)],
        out_shape=[jax.ShapeDtypeStruct((r, d_a), BF16), jax.ShapeDtypeStruct((r, d_a), BF16),
                   jax.ShapeDtypeStruct((r, d_b), F32)],
        compiler_params=_params(("parallel",)),
        name="mix0_in",
    )(x, mod, gains, w_in, norm_g)


def _s5_kernel(steps, uf_ref, ub_ref, bf_ref, bb_ref, cf_ref, cb_ref, lam_ref,
               yf_ref, yb_ref, sf_scr, sb_scr, hf_scr, hb_scr):
    nb = V7X_SUBLANES
    n = lam_ref.shape[1]

    @pl.when(pl.program_id(0) == 0)
    def _():
        hf_scr[...] = jnp.zeros_like(hf_scr)
        hb_scr[...] = jnp.zeros_like(hb_scr)

    sf_scr[...] = _mm(uf_ref[...].astype(BF16), bf_ref[...])
    sb_scr[...] = _mm(ub_ref[...].astype(BF16), bb_ref[...])

    def scan(s_scr, h_scr, lam_row, reverse):
        lr = lam_ref[lam_row:lam_row + nb, :]
        li = lam_ref[lam_row + nb:lam_row + 2 * nb, :]

        def step(k, carry):
            hr, hi = carry
            t = (steps - 1 - k) if reverse else k
            rows = pl.ds(pl.multiple_of(t * nb, nb), nb)
            nr = lr * hr - li * hi + s_scr[rows, 0:n]
            ni = lr * hi + li * hr + s_scr[rows, n:2 * n]
            s_scr[rows, 0:n] = nr
            s_scr[rows, n:2 * n] = ni
            return nr, ni

        hr, hi = lax.fori_loop(0, steps, step, (h_scr[:, 0:n], h_scr[:, n:2 * n]))
        h_scr[:, 0:n] = hr
        h_scr[:, n:2 * n] = hi

    scan(sf_scr, hf_scr, 0, False)
    scan(sb_scr, hb_scr, 2 * nb, True)
    yf_ref[...] = _mm(sf_scr[...].astype(BF16), cf_ref[...])
    yb_ref[...] = _mm(sb_scr[...].astype(BF16), cb_ref[...])


def _s5_scan(u_tm, n_ctx_steps, b_mats, c_mats, lam):
    rows, d_b = u_tm.shape
    nb = V7X_SUBLANES
    steps = _tiles()["s5_steps"]
    n_chunks = rows // (steps * nb)
    n_ctx = n_ctx_steps // steps
    n2 = b_mats.shape[-1]

    def bwd_chunk(k):
        return jnp.where(k < n_ctx, n_ctx - 1 - k, n_chunks - 1 - (k - n_ctx))

    blk = (steps * nb, d_b)
    return pl.pallas_call(
        functools.partial(_s5_kernel, steps),
        grid=(n_chunks,),
        in_specs=[pl.BlockSpec(blk, lambda k: (k, 0)),
                  pl.BlockSpec(blk, lambda k: (bwd_chunk(k), 0)),
                  pl.BlockSpec((None, d_b, n2), lambda k: (0, 0, 0)),
                  pl.BlockSpec((None, d_b, n2), lambda k: (1, 0, 0)),
                  pl.BlockSpec((None, n2, d_b), lambda k: (0, 0, 0)),
                  pl.BlockSpec((None, n2, d_b), lambda k: (1, 0, 0)),
                  _const_spec(lam.shape)],
        out_specs=[pl.BlockSpec(blk, lambda k: (k, 0)),
                   pl.BlockSpec(blk, lambda k: (bwd_chunk(k), 0))],
        out_shape=[jax.ShapeDtypeStruct((rows, d_b), F32)] * 2,
        scratch_shapes=[pltpu.VMEM((steps * nb, n2), F32), pltpu.VMEM((steps * nb, n2), F32),
                        pltpu.VMEM((nb, n2), F32), pltpu.VMEM((nb, n2), F32)],
        compiler_params=_params(("arbitrary",)),
        name="s5_scan",
    )(u_tm, u_tm, b_mats, b_mats, c_mats, c_mats, lam)


def _s5_operands(lam_re, lam_im, log_step, b_re, b_im, c_re, c_im):
    n_dir, groups, states = lam_re.shape
    gdim = b_re.shape[-1]
    dt = jnp.exp(log_step.astype(F32))[..., None]
    lr, li = lam_re.astype(F32), lam_im.astype(F32)
    mag = jnp.exp(lr * dt)
    ar, ai = mag * jnp.cos(li * dt), mag * jnp.sin(li * dt)
    den = lr * lr + li * li
    fr = ((ar - 1.0) * lr + ai * li) / den
    fi = (ai * lr - (ar - 1.0) * li) / den
    bbr = fr[..., None] * b_re - fi[..., None] * b_im
    bbi = fr[..., None] * b_im + fi[..., None] * b_re
    eye = jnp.eye(groups, dtype=F32)
    n = groups * states

    def in_mat(b):
        return jnp.einsum("dgpc,gh->dgchp", b, eye).reshape(n_dir, groups * gdim, n)

    def out_mat(c):
        return jnp.einsum("dgcp,gh->dgphc", c, eye).reshape(n_dir, n, groups * gdim)

    b_mats = jnp.concatenate([in_mat(bbr), in_mat(bbi)], axis=-1).astype(BF16)
    c_mats = jnp.concatenate([out_mat(c_re.astype(F32)), -out_mat(c_im.astype(F32))], axis=1).astype(BF16)
    nb = V7X_SUBLANES
    lam = jnp.concatenate([jnp.broadcast_to(v.reshape(1, n), (nb, n))
                           for v in (ar[0], ai[0], ar[1], ai[1])], axis=0)
    return b_mats, c_mats, lam


def _mix0_out_kernel(x_ref, mod_ref, g_ref, ug_ref, vn_ref, yf_ref, yb_ref, us_ref,
                     sw_ref, sb_ref, d_ref, gw_ref, gb_ref, wo_ref, o_ref, z_scr):
    tm, d_a = ug_ref.shape
    for n in range(tm // SGU_CHUNK):
        r0 = n * SGU_CHUNK
        for g in range(d_a // SGU_CHUNK):
            c0 = g * SGU_CHUNK
            mixed = _mm(sw_ref[g], vn_ref[r0:r0 + SGU_CHUNK, c0:c0 + SGU_CHUNK]) + sb_ref[g]
            z_scr[r0:r0 + SGU_CHUNK, c0:c0 + SGU_CHUNK] = (
                ug_ref[r0:r0 + SGU_CHUNK, c0:c0 + SGU_CHUNK].astype(F32) * mixed).astype(BF16)
    ys = jax.nn.gelu(yf_ref[...] + yb_ref[...] + d_ref[...] * us_ref[...])
    gl = jax.nn.sigmoid(_mm(ys.astype(BF16), gw_ref[...]) + gb_ref[...])
    z_scr[:, d_a:] = (ys * gl).astype(BF16)
    y = _mm(z_scr[...], wo_ref[...])
    o_ref[...] = x_ref[...] + mod_ref[5:6, :] * (_rms(y) * g_ref[1:2, :])


def _mix0_out(x, mod, gains, ug, vn, yf, yb, us, sgu_w, sgu_bias, d_skip, glu_w, glu_b, w_out, row_of):
    r, d = x.shape
    tm = _tiles()["mix"]
    rows = lambda a: pl.BlockSpec((tm, a.shape[1]), lambda i: (i, 0))
    consts = (sgu_w, sgu_bias, d_skip, glu_w, glu_b, w_out)
    return pl.pallas_call(
        _mix0_out_kernel,
        grid=(r // tm,),
        in_specs=[rows(x),
                  pl.BlockSpec((None, N_MOD_ROWS, d), lambda i: (row_of(i, tm), 0, 0)),
                  _const_spec(gains.shape), rows(ug), rows(vn), rows(yf), rows(yb), rows(us)]
                 + [_const_spec(c.shape) for c in consts],
        out_specs=rows(x),
        out_shape=jax.ShapeDtypeStruct((r, d), F32),
        scratch_shapes=[pltpu.VMEM((tm, d), BF16)],
        compiler_params=_params(("parallel",)),
        name="mix0_out",
    )(x, mod, gains, ug, vn, yf, yb, us, *consts)


def _rope(x, cos, sin_lo, sin_hi):
    quarter = HEAD_DIM // 4
    return (x * cos + pltpu.roll(x, HEAD_DIM - quarter, axis=1) * sin_lo
            + pltpu.roll(x, quarter, axis=1) * sin_hi)


def _modulated(x_ref, mod_ref, g_ref):
    return (_rms(x_ref[...]) * g_ref[0:1, :] * (1.0 + mod_ref[4:5, :]) + mod_ref[3:4, :]).astype(BF16)


def _kv_heads(h, w_ref, ng_ref, q_dim, k_ref, vt_ref, rope):
    kvh = vt_ref.shape[0]
    for hd in range(kvh):
        c0 = q_dim + hd * HEAD_DIM
        k = _rms(_mm(h, w_ref[:, c0:c0 + HEAD_DIM])) * ng_ref[1:2, :]
        if rope is not None:
            k = _rope(k, *rope)
        k_ref[:, hd * HEAD_DIM:(hd + 1) * HEAD_DIM] = k.astype(BF16)
        c0 = q_dim + (kvh + hd) * HEAD_DIM
        vt_ref[hd] = _mm(h, w_ref[:, c0:c0 + HEAD_DIM]).T.astype(BF16)


def _kv_ctx_kernel(x_ref, mod_ref, g_ref, w_ref, ng_ref, k_ref, vt_ref):
    q_dim = w_ref.shape[1] - 2 * k_ref.shape[1]
    _kv_heads(_modulated(x_ref, mod_ref, g_ref), w_ref, ng_ref, q_dim, k_ref, vt_ref, None)


def _qkv_lat_kernel(x_ref, mod_ref, g_ref, w_ref, ng_ref, rope_ref, k_in, vt_in, qt_ref, k_ref, vt_ref):
    del k_in, vt_in
    heads = qt_ref.shape[0]
    h = _modulated(x_ref, mod_ref, g_ref)
    rope = (rope_ref[0], rope_ref[1], rope_ref[2])
    q_gain = ng_ref[0:1, :] * (HEAD_DIM ** -0.5 * math.log2(math.e))
    for hd in range(heads):
        c0 = hd * HEAD_DIM
        q = _rms(_mm(h, w_ref[:, c0:c0 + HEAD_DIM])) * q_gain
        qt_ref[hd] = _rope(q, *rope).T.astype(BF16)
    _kv_heads(h, w_ref, ng_ref, heads * HEAD_DIM, k_ref, vt_ref, rope)


def _qkv(xl, xc, mod, gains, w_qkv, qk_gains, rope_tab, bsz):
    d = xl.shape[1]
    seq, n_ctx = xl.shape[0] // bsz, xc.shape[0] // bsz
    kv_dim = (w_qkv.shape[1] - d) // 2
    heads, kvh = d // HEAD_DIM, kv_dim // HEAD_DIM
    l_all = seq + n_ctx
    tm = _tiles()["mix"]
    pos_blocks = seq // tm
    ctx_blk = seq // n_ctx
    kv_shapes = [jax.ShapeDtypeStruct((bsz, l_all, kv_dim), BF16),
                 jax.ShapeDtypeStruct((bsz, kvh, HEAD_DIM, l_all), BF16)]
    consts = [_const_spec(gains.shape), _const_spec(w_qkv.shape), _const_spec(qk_gains.shape)]
    k_all, vt_all = pl.pallas_call(
        _kv_ctx_kernel,
        grid=(bsz,),
        in_specs=[pl.BlockSpec((n_ctx, d), lambda b: (b, 0)),
                  pl.BlockSpec((None, N_MOD_ROWS, d), lambda b: (CTX_MOD_ROW, 0, 0))] + consts,
        out_specs=[pl.BlockSpec((None, n_ctx, kv_dim), lambda b: (b, ctx_blk, 0)),
                   pl.BlockSpec((None, kvh, HEAD_DIM, n_ctx), lambda b: (b, 0, 0, ctx_blk))],
        out_shape=kv_shapes,
        compiler_params=_params(("parallel",)),
        name="kv_ctx",
    )(xc, mod, gains, w_qkv, qk_gains)
    any_spec = pl.BlockSpec(memory_space=pl.ANY)
    return pl.pallas_call(
        _qkv_lat_kernel,
        grid=(bsz * pos_blocks,),
        in_specs=[pl.BlockSpec((tm, d), lambda i: (i, 0)),
                  pl.BlockSpec((None, N_MOD_ROWS, d), lambda i: (i // pos_blocks, 0, 0))] + consts
                 + [pl.BlockSpec((3, tm, HEAD_DIM), lambda i: (0, i % pos_blocks, 0)), any_spec, any_spec],
        out_specs=[pl.BlockSpec((None, heads, HEAD_DIM, tm), lambda i: (i // pos_blocks, 0, 0, i % pos_blocks)),
                   pl.BlockSpec((None, tm, kv_dim), lambda i: (i // pos_blocks, i % pos_blocks, 0)),
                   pl.BlockSpec((None, kvh, HEAD_DIM, tm), lambda i: (i // pos_blocks, 0, 0, i % pos_blocks))],
        out_shape=[jax.ShapeDtypeStruct((bsz, heads, HEAD_DIM, seq), BF16)] + kv_shapes,
        input_output_aliases={6: 1, 7: 2},
        compiler_params=_params(("parallel",)),
        name="qkv",
    )(xl, mod, gains, w_qkv, qk_gains, rope_tab, k_all, vt_all)


def _rope_tables(seq):
    rows = seq // GRID_W
    axis_dim = HEAD_DIM // 2
    quarter = axis_dim // 2
    row_id = jnp.repeat(jnp.arange(rows, dtype=F32), GRID_W)
    col_id = jnp.tile(jnp.arange(GRID_W, dtype=F32), rows)
    inv_freq = ROPE_THETA ** (-jnp.arange(0, axis_dim, 2, dtype=F32) / axis_dim)
    a_row, a_col = row_id[:, None] * inv_freq, col_id[:, None] * inv_freq
    zero = jnp.zeros((seq, quarter), F32)
    cos = jnp.concatenate([jnp.cos(a_row)] * 2 + [jnp.cos(a_col)] * 2, axis=1)
    sin_lo = jnp.concatenate([-jnp.sin(a_row), zero, -jnp.sin(a_col), zero], axis=1)
    sin_hi = jnp.concatenate([zero, jnp.sin(a_row), zero, jnp.sin(a_col)], axis=1)
    return jnp.stack([cos, sin_lo, sin_hi])


def _attn_kernel(qt_ref, k_ref, vt_ref, o_ref, s_scr):
    heads = qt_ref.shape[0]
    n_keys = k_ref.shape[0]
    chunks = [(c, min(c + ATTN_KEY_CHUNK, n_keys)) for c in range(0, n_keys, ATTN_KEY_CHUNK)]

    def scores(h, c0, c1, m):
        s = _mm(k_ref[c0:c1, :], qt_ref[h])
        s_scr[h % 2, c0:c1, :] = s
        cm = jnp.max(s, axis=0, keepdims=True)
        return cm if m is None else jnp.maximum(m, cm)

    m = None
    for c0, c1 in chunks:
        m = scores(0, c0, c1, m)
    for h in range(heads):
        m_next, acc, l = None, None, None
        for c0, c1 in chunks:
            if h + 1 < heads:
                m_next = scores(h + 1, c0, c1, m_next)
            p = jnp.exp2(s_scr[h % 2, c0:c1, :] - m)
            ps = jnp.sum(p, axis=0, keepdims=True)
            pv = _mm(vt_ref[:, c0:c1], p.astype(BF16))
            l = ps if l is None else l + ps
            acc = pv if acc is None else acc + pv
        o_ref[:, h * HEAD_DIM:(h + 1) * HEAD_DIM] = (acc / l).T.astype(BF16)
        m = m_next


def _attention(qt, k, vt):
    bsz, heads, _, t = qt.shape
    l, kvh = k.shape[1], vt.shape[1]
    q_per_kv = heads // kvh
    tq = _tiles()["attn_q"]
    return pl.pallas_call(
        _attn_kernel,
        grid=(bsz, kvh, t // tq),
        in_specs=[pl.BlockSpec((None, q_per_kv, HEAD_DIM, tq), lambda b, j, i: (b, j, 0, i)),
                  pl.BlockSpec((None, l, HEAD_DIM), lambda b, j, i: (b, 0, j)),
                  pl.BlockSpec((None, None, HEAD_DIM, l), lambda b, j, i: (b, j, 0, 0))],
        out_specs=pl.BlockSpec((None, tq, q_per_kv * HEAD_DIM), lambda b, j, i: (b, i, j)),
        out_shape=jax.ShapeDtypeStruct((bsz, t, heads * HEAD_DIM), BF16),
        scratch_shapes=[pltpu.VMEM((2, l, tq), F32)],
        compiler_params=_params(("parallel",) * 3),
        name="attention",
    )(qt, k, vt)


def _attn_out_kernel(x_ref, mod_ref, g_ref, a_ref, wo_ref, o_ref):
    y = _mm(a_ref[...], wo_ref[...])
    o_ref[...] = x_ref[...] + mod_ref[5:6, :] * (_rms(y) * g_ref[1:2, :])


def _attn_out(x, mod, gains, a, w_out, row_of):
    r, d = x.shape
    tm = _tiles()["mix"]
    rows = lambda w: pl.BlockSpec((tm, w), lambda i: (i, 0))
    return pl.pallas_call(
        _attn_out_kernel,
        grid=(r // tm,),
        in_specs=[rows(d),
                  pl.BlockSpec((None, N_MOD_ROWS, d), lambda i: (row_of(i, tm), 0, 0)),
                  _const_spec(gains.shape), rows(a.shape[1]), _const_spec(w_out.shape)],
        out_specs=rows(d),
        out_shape=jax.ShapeDtypeStruct((r, d), F32),
        compiler_params=_params(("parallel",)),
        name="attn_out",
    )(x, mod, gains, a, w_out)


def kernel(x, c, ctx, c_ctx, w_mod, b_mod, norm_pre, norm_post, ffn_w_in, ffn_w_out, ab_w_in, ab_w_out, sgu_norm_g, sgu_w, sgu_b, s5_lam_re, s5_lam_im, s5_log_step, s5_b_re, s5_b_im, s5_c_re, s5_c_im, s5_d, s5_glu_w, s5_glu_b, attn_w_qkv, attn_w_out, attn_q_norm, attn_k_norm):
    bsz, seq, d = x.shape
    n_ctx = ctx.shape[1]
    depth = w_mod.shape[0]
    assert bsz == V7X_SUBLANES, "the S5 scan keeps the batch on the sublane axis"
    d_a = sgu_norm_g.shape[1]
    d_b = s5_d.shape[1]
    kv_dim = (attn_w_qkv.shape[2] - d) // 2
    q_per_kv = d // kv_dim

    def lat_row(i, tm):
        return (i * tm) // seq

    def ctx_row(i, tm):
        return CTX_MOD_ROW

    cond = jnp.zeros((N_MOD_ROWS, d), F32).at[:bsz].set(c).at[CTX_MOD_ROW].set(c_ctx)
    mod_all = _modulation(cond, w_mod, b_mod).reshape(depth, N_MOD_ROWS, -1, d)
    mod_all = jnp.pad(mod_all, ((0, 0), (0, 0), (0, N_MOD_ROWS - mod_all.shape[2]), (0, 0)))

    xl = x.reshape(bsz * seq, d)
    xc = ctx.reshape(bsz * n_ctx, d)
    streams = ((lat_row, seq), (ctx_row, n_ctx))

    for i in range(depth):
        last = i == depth - 1
        j = i // 2
        mod = mod_all[i]
        gains = [jnp.zeros((V7X_SUBLANES, d), F32).at[0].set(norm_pre[i, s]).at[1].set(norm_post[i, s])
                 for s in range(3)]
        w_in1, w_in2 = ffn_w_in[i, 0].astype(BF16), ffn_w_in[i, 1].astype(BF16)
        w_out1, w_out2 = ffn_w_out[i, 0].astype(BF16), ffn_w_out[i, 1].astype(BF16)

        xl = _ffn(xl, mod, gains[0], w_in1, w_out1, 0, lat_row)
        xc = _ffn(xc, mod, gains[0], w_in1, w_out1, 0, ctx_row)

        if i % 2 == 0:
            w_in = ab_w_in[j].astype(BF16)
            norm_g = sgu_norm_g[j].reshape(1, d_a)
            ug_l, vn_l, us_l = _mix0_in(xl, mod, gains[1], w_in, norm_g, d_a, lat_row)
            ug_c, vn_c, us_c = _mix0_in(xc, mod, gains[1], w_in, norm_g, d_a, ctx_row)
            u_tm = jnp.concatenate([us_c.reshape(bsz, n_ctx, d_b), us_l.reshape(bsz, seq, d_b)], axis=1)
            u_tm = u_tm.transpose(1, 0, 2).reshape((n_ctx + seq) * bsz, d_b)
            b_mats, c_mats, lam = _s5_operands(s5_lam_re[j], s5_lam_im[j], s5_log_step[j], s5_b_re[j],
                                               s5_b_im[j], s5_c_re[j], s5_c_im[j])
            yf, yb = _s5_scan(u_tm, n_ctx, b_mats, c_mats, lam)

            def batch_major(y):
                y = y.reshape(n_ctx + seq, bsz, d_b).transpose(1, 0, 2)
                return y[:, n_ctx:].reshape(bsz * seq, d_b), y[:, :n_ctx].reshape(bsz * n_ctx, d_b)

            (yf_l, yf_c), (yb_l, yb_c) = batch_major(yf), batch_major(yb)
            consts = (sgu_w[j].astype(BF16),
                      jnp.broadcast_to(sgu_b[j][:, :, None], sgu_w[j].shape).astype(F32),
                      s5_d[j].reshape(1, d_b), s5_glu_w[j].astype(BF16), s5_glu_b[j].reshape(1, d_b),
                      ab_w_out[j].astype(BF16))
            xl = _mix0_out(xl, mod, gains[1], ug_l, vn_l, yf_l, yb_l, us_l, *consts, lat_row)
            if not last:
                xc = _mix0_out(xc, mod, gains[1], ug_c, vn_c, yf_c, yb_c, us_c, *consts, ctx_row)
        else:
            if not last:
                raise NotImplementedError("context stream through an attention layer")
            w_qkv = attn_w_qkv[j].astype(BF16)
            qk_gains = jnp.zeros((V7X_SUBLANES, HEAD_DIM), F32).at[0].set(attn_q_norm[j]).at[1].set(attn_k_norm[j])
            rope_tab = _rope_tables(seq)
            qt, k, vt = _qkv(xl, xc, mod, gains[1], w_qkv, qk_gains, rope_tab, bsz)
            a = _attention(qt, k, vt).reshape(bsz * seq, d)
            xl = _attn_out(xl, mod, gains[1], a, attn_w_out[j].astype(BF16), lat_row)

        xl = _ffn(xl, mod, gains[2], w_in2, w_out2, 2, lat_row)
        if not last:
            xc = _ffn(xc, mod, gains[2], w_in2, w_out2, 2, ctx_row)

    return xl.reshape(bsz, seq, d)
```

```python
import functools
import math

import jax
import jax.numpy as jnp
from jax import lax
from jax.experimental import pallas as pl
from jax.experimental.pallas import tpu as pltpu

F32 = jnp.float32
BF16 = jnp.bfloat16

NORM_EPS = 1e-6
FFN_RES = 0.5
GRID_W = 64
ROPE_THETA = 10000.0
SGU_CHUNK = 128
HEAD_DIM = 128
ATTN_KEY_CHUNK = 512
V_ONES_ROWS = 16
V_ROWS = HEAD_DIM + V_ONES_ROWS

V7X_LANES = 128
V7X_SUBLANES = 8
MXU_COLS = 256
V7X_VMEM_BYTES = 64 * 1024 * 1024
VMEM_LIMIT = 56 * 1024 * 1024

N_MOD_ROWS = 16
CTX_MOD_ROW = 8


def _tiles():
    return dict(ffn=512, mix=512, attn_q=512, s5_steps=32, mod_n=2304)


def _params(sem, vmem=VMEM_LIMIT):
    return pltpu.CompilerParams(dimension_semantics=sem, vmem_limit_bytes=vmem)


def _const_spec(shape):
    nd = len(shape)
    return pl.BlockSpec(shape, lambda *_: (0,) * nd)


def _rms(x):
    return x * lax.rsqrt(jnp.mean(x * x, axis=-1, keepdims=True) + NORM_EPS)


def _mm(a, b):
    return jnp.dot(a, b, preferred_element_type=F32)


def _mod_kernel(c_ref, w_ref, b_ref, o_ref):
    c = c_ref[...]
    s = (c * jax.nn.sigmoid(c)).astype(BF16)
    o_ref[...] = _mm(s, w_ref[...].astype(BF16)) + b_ref[...]


def _modulation(cond, w_mod, b_mod):
    depth, d, n = w_mod.shape
    tn = _tiles()["mod_n"]
    return pl.pallas_call(
        _mod_kernel,
        grid=(depth, n // tn),
        in_specs=[_const_spec((N_MOD_ROWS, d)),
                  pl.BlockSpec((None, d, tn), lambda l, j: (l, 0, j)),
                  pl.BlockSpec((None, 1, tn), lambda l, j: (l, 0, j))],
        out_specs=pl.BlockSpec((None, N_MOD_ROWS, tn), lambda l, j: (l, 0, j)),
        out_shape=jax.ShapeDtypeStruct((depth, N_MOD_ROWS, n), F32),
        compiler_params=_params(("parallel", "parallel")),
        name="modulation",
    )(cond, w_mod, b_mod.reshape(depth, 1, n))


def _ffn_chunks(d_ff, width=512):
    return [(c, min(c + width, d_ff)) for c in range(0, d_ff, width)]


def _ffn_kernel(sub, x_ref, mod_ref, g_ref, win_ref, wout_ref, o_ref, a_scr):
    d_ff = wout_ref.shape[0]
    x = x_ref[...]
    shift, scale, gate = (mod_ref[3 * sub + k:3 * sub + k + 1, :] for k in range(3))
    g_pre, g_post = g_ref[0:1, :], g_ref[1:2, :]
    h = (_rms(x) * g_pre * (1.0 + scale) + shift).astype(BF16)
    for c0, c1 in _ffn_chunks(d_ff):
        g = _mm(h, win_ref[:, c0:c1])
        u = _mm(h, win_ref[:, d_ff + c0:d_ff + c1])
        a_scr[:, c0:c1] = (g * jax.nn.sigmoid(g) * u).astype(BF16)
    y = _mm(a_scr[...], wout_ref[...])
    o_ref[...] = x + (FFN_RES * gate) * (_rms(y) * g_post)


def _ffn(x, mod, gains, w_in, w_out, sub, row_of):
    r, d = x.shape
    d_ff = w_out.shape[0]
    tm = _tiles()["ffn"]
    return pl.pallas_call(
        functools.partial(_ffn_kernel, sub),
        grid=(r // tm,),
        in_specs=[pl.BlockSpec((tm, d), lambda i: (i, 0)),
                  pl.BlockSpec((None, N_MOD_ROWS, d), lambda i: (row_of(i, tm), 0, 0)),
                  _const_spec(gains.shape),
                  _const_spec(w_in.shape),
                  _const_spec(w_out.shape)],
        out_specs=pl.BlockSpec((tm, d), lambda i: (i, 0)),
        out_shape=jax.ShapeDtypeStruct((r, d), F32),
        scratch_shapes=[pltpu.VMEM((tm, d_ff), BF16)],
        compiler_params=_params(("parallel",)),
        name=f"ffn{sub}",
    )(x, mod, gains, w_in, w_out)


def _mix0_in_kernel(x_ref, mod_ref, g_ref, w_ref, ng_ref, ug_ref, vn_ref, us_ref):
    d_a = ug_ref.shape[1]
    x = x_ref[...]
    shift, scale = mod_ref[3:4, :], mod_ref[4:5, :]
    h = (_rms(x) * g_ref[0:1, :] * (1.0 + scale) + shift).astype(BF16)
    ug_ref[...] = jax.nn.gelu(_mm(h, w_ref[:, 0:d_a])).astype(BF16)
    for c0 in range(0, d_a, MXU_COLS):
        vv = jax.nn.gelu(_mm(h, w_ref[:, d_a + c0:d_a + c0 + MXU_COLS]))
        for g0 in range(0, MXU_COLS, SGU_CHUNK):
            v = vv[:, g0:g0 + SGU_CHUNK]
            mu = jnp.mean(v, axis=-1, keepdims=True)
            vc = v - mu
            var = jnp.mean(vc * vc, axis=-1, keepdims=True)
            vn_ref[:, c0 + g0:c0 + g0 + SGU_CHUNK] = (
                vc * lax.rsqrt(var + NORM_EPS) * ng_ref[:, c0 + g0:c0 + g0 + SGU_CHUNK]).astype(BF16)
    us_ref[...] = _mm(h, w_ref[:, 2 * d_a:])


def _mix0_in(x, mod, gains, w_in, norm_g, d_a, row_of):
    r, d = x.shape
    d_b = w_in.shape[1] - 2 * d_a
    tm = _tiles()["mix"]
    rows = lambda w: pl.BlockSpec((tm, w), lambda i: (i, 0))
    return pl.pallas_call(
        _mix0_in_kernel,
        grid=(r // tm,),
        in_specs=[rows(d),
                  pl.BlockSpec((None, N_MOD_ROWS, d), lambda i: (row_of(i, tm), 0, 0)),
                  _const_spec(gains.shape), _const_spec(w_in.shape), _const_spec(norm_g.shape)],
        out_specs=[rows(d_a), rows(d_a), rows(d_b)],
        out_shape=[jax.ShapeDtypeStruct((r, d_a), BF16), jax.ShapeDtypeStruct((r, d_a), BF16),
                   jax.ShapeDtypeStruct((r, d_b), F32)],
        compiler_params=_params(("parallel",)),
        name="mix0_in",
    )(x, mod, gains, w_in, norm_g)


def _s5_kernel(steps, uf_ref, ub_ref, bf_ref, bb_ref, cf_ref, cb_ref, lam_ref,
               yf_ref, yb_ref, sf_scr, sb_scr, hf_scr, hb_scr):
    nb = V7X_SUBLANES
    n = lam_ref.shape[1]

    @pl.when(pl.program_id(0) == 0)
    def _():
        hf_scr[...] = jnp.zeros_like(hf_scr)
        hb_scr[...] = jnp.zeros_like(hb_scr)

    sf_scr[...] = _mm(uf_ref[...].astype(BF16), bf_ref[...])
    sb_scr[...] = _mm(ub_ref[...].astype(BF16), bb_ref[...])

    def scan(s_scr, h_scr, lam_row, reverse):
        lr = lam_ref[lam_row:lam_row + nb, :]
        li = lam_ref[lam_row + nb:lam_row + 2 * nb, :]

        def step(k, carry):
            hr, hi = carry
            t = (steps - 1 - k) if reverse else k
            rows = pl.ds(pl.multiple_of(t * nb, nb), nb)
            nr = lr * hr - li * hi + s_scr[rows, 0:n]
            ni = lr * hi + li * hr + s_scr[rows, n:2 * n]
            s_scr[rows, 0:n] = nr
            s_scr[rows, n:2 * n] = ni
            return nr, ni

        hr, hi = lax.fori_loop(0, steps, step, (h_scr[:, 0:n], h_scr[:, n:2 * n]))
        h_scr[:, 0:n] = hr
        h_scr[:, n:2 * n] = hi

    scan(sf_scr, hf_scr, 0, False)
    scan(sb_scr, hb_scr, 2 * nb, True)
    yf_ref[...] = _mm(sf_scr[...].astype(BF16), cf_ref[...])
    yb_ref[...] = _mm(sb_scr[...].astype(BF16), cb_ref[...])


def _s5_scan(u_tm, n_ctx_steps, b_mats, c_mats, lam):
    rows, d_b = u_tm.shape
    nb = V7X_SUBLANES
    steps = _tiles()["s5_steps"]
    n_chunks = rows // (steps * nb)
    n_ctx = n_ctx_steps // steps
    n2 = b_mats.shape[-1]

    def bwd_chunk(k):
        return jnp.where(k < n_ctx, n_ctx - 1 - k, n_chunks - 1 - (k - n_ctx))

    blk = (steps * nb, d_b)
    return pl.pallas_call(
        functools.partial(_s5_kernel, steps),
        grid=(n_chunks,),
        in_specs=[pl.BlockSpec(blk, lambda k: (k, 0)),
                  pl.BlockSpec(blk, lambda k: (bwd_chunk(k), 0)),
                  pl.BlockSpec((None, d_b, n2), lambda k: (0, 0, 0)),
                  pl.BlockSpec((None, d_b, n2), lambda k: (1, 0, 0)),
                  pl.BlockSpec((None, n2, d_b), lambda k: (0, 0, 0)),
                  pl.BlockSpec((None, n2, d_b), lambda k: (1, 0, 0)),
                  _const_spec(lam.shape)],
        out_specs=[pl.BlockSpec(blk, lambda k: (k, 0)),
                   pl.BlockSpec(blk, lambda k: (bwd_chunk(k), 0))],
        out_shape=[jax.ShapeDtypeStruct((rows, d_b), F32)] * 2,
        scratch_shapes=[pltpu.VMEM((steps * nb, n2), F32), pltpu.VMEM((steps * nb, n2), F32),
                        pltpu.VMEM((nb, n2), F32), pltpu.VMEM((nb, n2), F32)],
        compiler_params=_params(("arbitrary",)),
        name="s5_scan",
    )(u_tm, u_tm, b_mats, b_mats, c_mats, c_mats, lam)


def _s5_operands(lam_re, lam_im, log_step, b_re, b_im, c_re, c_im):
    n_dir, groups, states = lam_re.shape
    gdim = b_re.shape[-1]
    dt = jnp.exp(log_step.astype(F32))[..., None]
    lr, li = lam_re.astype(F32), lam_im.astype(F32)
    mag = jnp.exp(lr * dt)
    ar, ai = mag * jnp.cos(li * dt), mag * jnp.sin(li * dt)
    den = lr * lr + li * li
    fr = ((ar - 1.0) * lr + ai * li) / den
    fi = (ai * lr - (ar - 1.0) * li) / den
    bbr = fr[..., None] * b_re - fi[..., None] * b_im
    bbi = fr[..., None] * b_im + fi[..., None] * b_re
    eye = jnp.eye(groups, dtype=F32)
    n = groups * states

    def in_mat(b):
        return jnp.einsum("dgpc,gh->dgchp", b, eye).reshape(n_dir, groups * gdim, n)

    def out_mat(c):
        return jnp.einsum("dgcp,gh->dgphc", c, eye).reshape(n_dir, n, groups * gdim)

    b_mats = jnp.concatenate([in_mat(bbr), in_mat(bbi)], axis=-1).astype(BF16)
    c_mats = jnp.concatenate([out_mat(c_re.astype(F32)), -out_mat(c_im.astype(F32))], axis=1).astype(BF16)
    nb = V7X_SUBLANES
    lam = jnp.concatenate([jnp.broadcast_to(v.reshape(1, n), (nb, n))
                           for v in (ar[0], ai[0], ar[1], ai[1])], axis=0)
    return b_mats, c_mats, lam


def _mix0_out_kernel(x_ref, mod_ref, g_ref, ug_ref, vn_ref, yf_ref, yb_ref, us_ref,
                     sw_ref, sb_ref, d_ref, gw_ref, gb_ref, wo_ref, o_ref, z_scr):
    tm, d_a = ug_ref.shape
    row_chunks = range(0, tm, SGU_CHUNK)
    for g in range(d_a // SGU_CHUNK):
        c0 = g * SGU_CHUNK
        vn_wide = jnp.concatenate([vn_ref[r0:r0 + SGU_CHUNK, c0:c0 + SGU_CHUNK] for r0 in row_chunks], axis=1)
        mixed_wide = _mm(sw_ref[g], vn_wide)
        for n, r0 in enumerate(row_chunks):
            mixed = mixed_wide[:, n * SGU_CHUNK:(n + 1) * SGU_CHUNK] + sb_ref[g]
            z_scr[r0:r0 + SGU_CHUNK, c0:c0 + SGU_CHUNK] = (
                ug_ref[r0:r0 + SGU_CHUNK, c0:c0 + SGU_CHUNK].astype(F32) * mixed).astype(BF16)
    ys = jax.nn.gelu(yf_ref[...] + yb_ref[...] + d_ref[...] * us_ref[...])
    gl = jax.nn.sigmoid(_mm(ys.astype(BF16), gw_ref[...]) + gb_ref[...])
    z_scr[:, d_a:] = (ys * gl).astype(BF16)
    y = _mm(z_scr[...], wo_ref[...])
    o_ref[...] = x_ref[...] + mod_ref[5:6, :] * (_rms(y) * g_ref[1:2, :])


def _mix0_out(x, mod, gains, ug, vn, yf, yb, us, sgu_w, sgu_bias, d_skip, glu_w, glu_b, w_out, row_of):
    r, d = x.shape
    tm = _tiles()["mix"]
    rows = lambda a: pl.BlockSpec((tm, a.shape[1]), lambda i: (i, 0))
    consts = (sgu_w, sgu_bias, d_skip, glu_w, glu_b, w_out)
    return pl.pallas_call(
        _mix0_out_kernel,
        grid=(r // tm,),
        in_specs=[rows(x),
                  pl.BlockSpec((None, N_MOD_ROWS, d), lambda i: (row_of(i, tm), 0, 0)),
                  _const_spec(gains.shape), rows(ug), rows(vn), rows(yf), rows(yb), rows(us)]
                 + [_const_spec(c.shape) for c in consts],
        out_specs=rows(x),
        out_shape=jax.ShapeDtypeStruct((r, d), F32),
        scratch_shapes=[pltpu.VMEM((tm, d), BF16)],
        compiler_params=_params(("parallel",)),
        name="mix0_out",
    )(x, mod, gains, ug, vn, yf, yb, us, *consts)


def _rope(x, cos, sin_lo, sin_hi):
    quarter = HEAD_DIM // 4
    return (x * cos + pltpu.roll(x, HEAD_DIM - quarter, axis=1) * sin_lo
            + pltpu.roll(x, quarter, axis=1) * sin_hi)


def _modulated(x_ref, mod_ref, g_ref):
    return (_rms(x_ref[...]) * g_ref[0:1, :] * (1.0 + mod_ref[4:5, :]) + mod_ref[3:4, :]).astype(BF16)


def _head_slices(h, w_ref, col0, n_heads):
    per_dot = MXU_COLS // HEAD_DIM
    assert n_heads % per_dot == 0
    for pair in range(n_heads // per_dot):
        c0 = col0 + pair * MXU_COLS
        wide = _mm(h, w_ref[:, c0:c0 + MXU_COLS])
        for sub in range(per_dot):
            yield wide[:, sub * HEAD_DIM:(sub + 1) * HEAD_DIM]


def _kv_heads(h, w_ref, ng_ref, q_dim, k_ref, vt_ref, rope):
    kvh = vt_ref.shape[0]
    tokens = vt_ref.shape[2]
    for hd, k in enumerate(_head_slices(h, w_ref, q_dim, kvh)):
        k = _rms(k) * ng_ref[1:2, :]
        if rope is not None:
            k = _rope(k, *rope)
        k_ref[:, hd * HEAD_DIM:(hd + 1) * HEAD_DIM] = k.astype(BF16)
    for hd, v in enumerate(_head_slices(h, w_ref, q_dim + kvh * HEAD_DIM, kvh)):
        vt_ref[hd, 0:HEAD_DIM, :] = v.T.astype(BF16)
        vt_ref[hd, HEAD_DIM:, :] = jnp.ones((V_ONES_ROWS, tokens), BF16)


def _kv_ctx_kernel(x_ref, mod_ref, g_ref, w_ref, ng_ref, k_ref, vt_ref):
    q_dim = w_ref.shape[1] - 2 * k_ref.shape[1]
    _kv_heads(_modulated(x_ref, mod_ref, g_ref), w_ref, ng_ref, q_dim, k_ref, vt_ref, None)


def _qkv_lat_kernel(x_ref, mod_ref, g_ref, w_ref, ng_ref, rope_ref, k_in, vt_in, qt_ref, k_ref, vt_ref):
    del k_in, vt_in
    heads = qt_ref.shape[0]
    h = _modulated(x_ref, mod_ref, g_ref)
    rope = (rope_ref[0], rope_ref[1], rope_ref[2])
    q_gain = ng_ref[0:1, :] * (HEAD_DIM ** -0.5 * math.log2(math.e))
    for hd, q in enumerate(_head_slices(h, w_ref, 0, heads)):
        qt_ref[hd] = _rope(_rms(q) * q_gain, *rope).T.astype(BF16)
    _kv_heads(h, w_ref, ng_ref, heads * HEAD_DIM, k_ref, vt_ref, rope)


def _qkv(xl, xc, mod, gains, w_qkv, qk_gains, rope_tab, bsz):
    d = xl.shape[1]
    seq, n_ctx = xl.shape[0] // bsz, xc.shape[0] // bsz
    kv_dim = (w_qkv.shape[1] - d) // 2
    heads, kvh = d // HEAD_DIM, kv_dim // HEAD_DIM
    l_all = seq + n_ctx
    tm = _tiles()["mix"]
    pos_blocks = seq // tm
    ctx_blk = seq // n_ctx
    kv_shapes = [jax.ShapeDtypeStruct((bsz, l_all, kv_dim), BF16),
                 jax.ShapeDtypeStruct((bsz, kvh, V_ROWS, l_all), BF16)]
    consts = [_const_spec(gains.shape), _const_spec(w_qkv.shape), _const_spec(qk_gains.shape)]
    k_all, vt_all = pl.pallas_call(
        _kv_ctx_kernel,
        grid=(bsz,),
        in_specs=[pl.BlockSpec((n_ctx, d), lambda b: (b, 0)),
                  pl.BlockSpec((None, N_MOD_ROWS, d), lambda b: (CTX_MOD_ROW, 0, 0))] + consts,
        out_specs=[pl.BlockSpec((None, n_ctx, kv_dim), lambda b: (b, ctx_blk, 0)),
                   pl.BlockSpec((None, kvh, V_ROWS, n_ctx), lambda b: (b, 0, 0, ctx_blk))],
        out_shape=kv_shapes,
        compiler_params=_params(("parallel",)),
        name="kv_ctx",
    )(xc, mod, gains, w_qkv, qk_gains)
    any_spec = pl.BlockSpec(memory_space=pl.ANY)
    return pl.pallas_call(
        _qkv_lat_kernel,
        grid=(bsz * pos_blocks,),
        in_specs=[pl.BlockSpec((tm, d), lambda i: (i, 0)),
                  pl.BlockSpec((None, N_MOD_ROWS, d), lambda i: (i // pos_blocks, 0, 0))] + consts
                 + [pl.BlockSpec((3, tm, HEAD_DIM), lambda i: (0, i % pos_blocks, 0)), any_spec, any_spec],
        out_specs=[pl.BlockSpec((None, heads, HEAD_DIM, tm), lambda i: (i // pos_blocks, 0, 0, i % pos_blocks)),
                   pl.BlockSpec((None, tm, kv_dim), lambda i: (i // pos_blocks, i % pos_blocks, 0)),
                   pl.BlockSpec((None, kvh, V_ROWS, tm), lambda i: (i // pos_blocks, 0, 0, i % pos_blocks))],
        out_shape=[jax.ShapeDtypeStruct((bsz, heads, HEAD_DIM, seq), BF16)] + kv_shapes,
        input_output_aliases={6: 1, 7: 2},
        compiler_params=_params(("parallel",)),
        name="qkv",
    )(xl, mod, gains, w_qkv, qk_gains, rope_tab, k_all, vt_all)


def _rope_tables(seq):
    rows = seq // GRID_W
    axis_dim = HEAD_DIM // 2
    quarter = axis_dim // 2
    row_id = jnp.repeat(jnp.arange(rows, dtype=F32), GRID_W)
    col_id = jnp.tile(jnp.arange(GRID_W, dtype=F32), rows)
    inv_freq = ROPE_THETA ** (-jnp.arange(0, axis_dim, 2, dtype=F32) / axis_dim)
    a_row, a_col = row_id[:, None] * inv_freq, col_id[:, None] * inv_freq
    zero = jnp.zeros((seq, quarter), F32)
    cos = jnp.concatenate([jnp.cos(a_row)] * 2 + [jnp.cos(a_col)] * 2, axis=1)
    sin_lo = jnp.concatenate([-jnp.sin(a_row), zero, -jnp.sin(a_col), zero], axis=1)
    sin_hi = jnp.concatenate([zero, jnp.sin(a_row), zero, jnp.sin(a_col)], axis=1)
    return jnp.stack([cos, sin_lo, sin_hi])


def _attn_kernel(qt_ref, qn_ref, k_ref, kn_ref, vt_ref, o_ref, s_scr, m_scr):
    heads = qt_ref.shape[0]
    n_keys = k_ref.shape[0]
    chunks = [(c, min(c + ATTN_KEY_CHUNK, n_keys)) for c in range(0, n_keys, ATTN_KEY_CHUNK)]

    def scores(q, keys_ref, slot, c0, c1, m):
        s = _mm(keys_ref[c0:c1, :], q)
        s_scr[slot, c0:c1, :] = s
        cm = jnp.max(s, axis=0, keepdims=True)
        return cm if m is None else jnp.maximum(m, cm)

    @pl.when((pl.program_id(0) == 0) & (pl.program_id(1) == 0) & (pl.program_id(2) == 0))
    def _():
        m0 = None
        for c0, c1 in chunks:
            m0 = scores(qt_ref[0], k_ref, 0, c0, c1, m0)
        m_scr[...] = m0

    m = m_scr[...]
    for h in range(heads):
        last = h + 1 == heads
        q_next, keys_next = (qn_ref[...], kn_ref) if last else (qt_ref[h + 1], k_ref)
        m_next, acc = None, None
        for c0, c1 in chunks:
            m_next = scores(q_next, keys_next, (h + 1) % 2, c0, c1, m_next)
            p = jnp.exp2(s_scr[h % 2, c0:c1, :] - m).astype(BF16)
            pv = _mm(vt_ref[:, c0:c1], p)
            acc = pv if acc is None else acc + pv
        o = acc[0:HEAD_DIM, :] / acc[HEAD_DIM:HEAD_DIM + 1, :]
        o_ref[:, h * HEAD_DIM:(h + 1) * HEAD_DIM] = o.T.astype(BF16)
        m = m_next
    m_scr[...] = m


def _attention(qt, k, vt):
    bsz, heads, _, t = qt.shape
    l, kvh = k.shape[1], vt.shape[1]
    q_per_kv = heads // kvh
    assert q_per_kv % 2 == 0, "score buffers alternate per head and must line up across grid steps"
    tq = _tiles()["attn_q"]
    nq = t // tq
    steps = bsz * kvh * nq

    def following(b, j, i):
        lin = jnp.minimum((b * kvh + j) * nq + i + 1, steps - 1)
        return lin // (kvh * nq), (lin // nq) % kvh, lin % nq

    def q_next_map(b, j, i):
        b2, j2, i2 = following(b, j, i)
        return b2, j2 * q_per_kv, 0, i2

    def k_next_map(b, j, i):
        b2, j2, _ = following(b, j, i)
        return b2, 0, j2

    return pl.pallas_call(
        _attn_kernel,
        grid=(bsz, kvh, nq),
        in_specs=[pl.BlockSpec((None, q_per_kv, HEAD_DIM, tq), lambda b, j, i: (b, j, 0, i)),
                  pl.BlockSpec((None, None, HEAD_DIM, tq), q_next_map),
                  pl.BlockSpec((None, l, HEAD_DIM), lambda b, j, i: (b, 0, j)),
                  pl.BlockSpec((None, l, HEAD_DIM), k_next_map),
                  pl.BlockSpec((None, None, V_ROWS, l), lambda b, j, i: (b, j, 0, 0))],
        out_specs=pl.BlockSpec((None, tq, q_per_kv * HEAD_DIM), lambda b, j, i: (b, i, j)),
        out_shape=jax.ShapeDtypeStruct((bsz, t, heads * HEAD_DIM), BF16),
        scratch_shapes=[pltpu.VMEM((2, l, tq), F32), pltpu.VMEM((1, tq), F32)],
        compiler_params=_params(("arbitrary",) * 3),
        name="attention",
    )(qt, qt, k, k, vt)


def _attn_out_kernel(x_ref, mod_ref, g_ref, a_ref, wo_ref, o_ref):
    y = _mm(a_ref[...], wo_ref[...])
    o_ref[...] = x_ref[...] + mod_ref[5:6, :] * (_rms(y) * g_ref[1:2, :])


def _attn_out(x, mod, gains, a, w_out, row_of):
    r, d = x.shape
    tm = _tiles()["mix"]
    rows = lambda w: pl.BlockSpec((tm, w), lambda i: (i, 0))
    return pl.pallas_call(
        _attn_out_kernel,
        grid=(r // tm,),
        in_specs=[rows(d),
                  pl.BlockSpec((None, N_MOD_ROWS, d), lambda i: (row_of(i, tm), 0, 0)),
                  _const_spec(gains.shape), rows(a.shape[1]), _const_spec(w_out.shape)],
        out_specs=rows(d),
        out_shape=jax.ShapeDtypeStruct((r, d), F32),
        compiler_params=_params(("parallel",)),
        name="attn_out",
    )(x, mod, gains, a, w_out)


def kernel(x, c, ctx, c_ctx, w_mod, b_mod, norm_pre, norm_post, ffn_w_in, ffn_w_out, ab_w_in, ab_w_out, sgu_norm_g, sgu_w, sgu_b, s5_lam_re, s5_lam_im, s5_log_step, s5_b_re, s5_b_im, s5_c_re, s5_c_im, s5_d, s5_glu_w, s5_glu_b, attn_w_qkv, attn_w_out, attn_q_norm, attn_k_norm):
    bsz, seq, d = x.shape
    n_ctx = ctx.shape[1]
    depth = w_mod.shape[0]
    assert bsz == V7X_SUBLANES, "the S5 scan keeps the batch on the sublane axis"
    d_a = sgu_norm_g.shape[1]
    d_b = s5_d.shape[1]
    kv_dim = (attn_w_qkv.shape[2] - d) // 2
    q_per_kv = d // kv_dim

    def lat_row(i, tm):
        return (i * tm) // seq

    def ctx_row(i, tm):
        return CTX_MOD_ROW

    cond = jnp.zeros((N_MOD_ROWS, d), F32).at[:bsz].set(c).at[CTX_MOD_ROW].set(c_ctx)
    mod_all = _modulation(cond, w_mod, b_mod).reshape(depth, N_MOD_ROWS, -1, d)
    mod_all = jnp.pad(mod_all, ((0, 0), (0, 0), (0, N_MOD_ROWS - mod_all.shape[2]), (0, 0)))

    xl = x.reshape(bsz * seq, d)
    xc = ctx.reshape(bsz * n_ctx, d)
    streams = ((lat_row, seq), (ctx_row, n_ctx))

    for i in range(depth):
        last = i == depth - 1
        j = i // 2
        mod = mod_all[i]
        gains = [jnp.zeros((V7X_SUBLANES, d), F32).at[0].set(norm_pre[i, s]).at[1].set(norm_post[i, s])
                 for s in range(3)]
        w_in1, w_in2 = ffn_w_in[i, 0].astype(BF16), ffn_w_in[i, 1].astype(BF16)
        w_out1, w_out2 = ffn_w_out[i, 0].astype(BF16), ffn_w_out[i, 1].astype(BF16)

        xl = _ffn(xl, mod, gains[0], w_in1, w_out1, 0, lat_row)
        xc = _ffn(xc, mod, gains[0], w_in1, w_out1, 0, ctx_row)

        if i % 2 == 0:
            w_in = ab_w_in[j].astype(BF16)
            norm_g = sgu_norm_g[j].reshape(1, d_a)
            ug_l, vn_l, us_l = _mix0_in(xl, mod, gains[1], w_in, norm_g, d_a, lat_row)
            ug_c, vn_c, us_c = _mix0_in(xc, mod, gains[1], w_in, norm_g, d_a, ctx_row)
            u_tm = jnp.concatenate([us_c.reshape(bsz, n_ctx, d_b), us_l.reshape(bsz, seq, d_b)], axis=1)
            u_tm = u_tm.transpose(1, 0, 2).reshape((n_ctx + seq) * bsz, d_b)
            b_mats, c_mats, lam = _s5_operands(s5_lam_re[j], s5_lam_im[j], s5_log_step[j], s5_b_re[j],
                                               s5_b_im[j], s5_c_re[j], s5_c_im[j])
            yf, yb = _s5_scan(u_tm, n_ctx, b_mats, c_mats, lam)

            def batch_major(y):
                y = y.reshape(n_ctx + seq, bsz, d_b).transpose(1, 0, 2)
                return y[:, n_ctx:].reshape(bsz * seq, d_b), y[:, :n_ctx].reshape(bsz * n_ctx, d_b)

            (yf_l, yf_c), (yb_l, yb_c) = batch_major(yf), batch_major(yb)
            consts = (sgu_w[j].astype(BF16),
                      jnp.broadcast_to(sgu_b[j][:, :, None], sgu_w[j].shape).astype(F32),
                      s5_d[j].reshape(1, d_b), s5_glu_w[j].astype(BF16), s5_glu_b[j].reshape(1, d_b),
                      ab_w_out[j].astype(BF16))
            xl = _mix0_out(xl, mod, gains[1], ug_l, vn_l, yf_l, yb_l, us_l, *consts, lat_row)
            if not last:
                xc = _mix0_out(xc, mod, gains[1], ug_c, vn_c, yf_c, yb_c, us_c, *consts, ctx_row)
        else:
            if not last:
                raise NotImplementedError("context stream through an attention layer")
            w_qkv = attn_w_qkv[j].astype(BF16)
            qk_gains = jnp.zeros((V7X_SUBLANES, HEAD_DIM), F32).at[0].set(attn_q_norm[j]).at[1].set(attn_k_norm[j])
            rope_tab = _rope_tables(seq)
            qt, k, vt = _qkv(xl, xc, mod, gains[1], w_qkv, qk_gains, rope_tab, bsz)
            a = _attention(qt, k, vt).reshape(bsz * seq, d)
            xl = _attn_out(xl, mod, gains[1], a, attn_w_out[j].astype(BF16), lat_row)

        xl = _ffn(xl, mod, gains[2], w_in2, w_out2, 2, lat_row)
        if not last:
            xc = _ffn(xc, mod, gains[2], w_in2, w_out2, 2, ctx_row)

    return xl.reshape(bsz, seq, d)
```

```python
import functools
import math

import jax
import jax.numpy as jnp
from jax import lax
from jax.experimental import pallas as pl
from jax.experimental.pallas import tpu as pltpu

F32 = jnp.float32
BF16 = jnp.bfloat16

NORM_EPS = 1e-6
FFN_RES = 0.5
GRID_W = 64
ROPE_THETA = 10000.0
SGU_CHUNK = 128
HEAD_DIM = 128
ATTN_KEY_CHUNK = 512
V_ONES_ROWS = 16
V_ROWS = HEAD_DIM + V_ONES_ROWS

V7X_LANES = 128
V7X_SUBLANES = 8
MXU_COLS = 256
V7X_VMEM_BYTES = 64 * 1024 * 1024
VMEM_LIMIT = 56 * 1024 * 1024

N_MOD_ROWS = 16
CTX_MOD_ROW = 8


def _tiles():
    return dict(ffn=(1024, 4), ffn_mix=(512, 2), mix=512, attn_q=512, s5_steps=64, mod_n=2304)


def _params(sem, vmem=VMEM_LIMIT):
    return pltpu.CompilerParams(dimension_semantics=sem, vmem_limit_bytes=vmem)


def _const_spec(shape):
    nd = len(shape)
    return pl.BlockSpec(shape, lambda *_: (0,) * nd)


def _resident_spec(shape):
    nd = len(shape)
    return pl.BlockSpec(shape, lambda *_: (0,) * nd, pipeline_mode=pl.Buffered(1))


def _rms(x):
    return x * lax.rsqrt(jnp.mean(x * x, axis=-1, keepdims=True) + NORM_EPS)


def _mm(a, b):
    return jnp.dot(a, b, preferred_element_type=F32)


def _mod_kernel(c_ref, w_ref, b_ref, o_ref):
    c = c_ref[...]
    s = (c * jax.nn.sigmoid(c)).astype(BF16)
    o_ref[...] = _mm(s, w_ref[...].astype(BF16)) + b_ref[...]


def _modulation(cond, w_mod, b_mod):
    depth, d, n = w_mod.shape
    tn = _tiles()["mod_n"]
    return pl.pallas_call(
        _mod_kernel,
        grid=(depth, n // tn),
        in_specs=[_const_spec((N_MOD_ROWS, d)),
                  pl.BlockSpec((None, d, tn), lambda l, j: (l, 0, j)),
                  pl.BlockSpec((None, 1, tn), lambda l, j: (l, 0, j))],
        out_specs=pl.BlockSpec((None, N_MOD_ROWS, tn), lambda l, j: (l, 0, j)),
        out_shape=jax.ShapeDtypeStruct((depth, N_MOD_ROWS, n), F32),
        compiler_params=_params(("parallel", "parallel")),
        name="modulation",
    )(cond, w_mod, b_mod.reshape(depth, 1, n))


def _ffn_chunks(d_ff, width=512):
    return [(c, min(c + width, d_ff)) for c in range(0, d_ff, width)]


def _attn_mix_out(r, x_ref, mod_ref, g_ref, mix_refs, z_scr):
    a_ref, wo_ref = mix_refs
    y = _mm(a_ref[r, :], wo_ref[...])
    return x_ref[r, :] + mod_ref[5:6, :] * (_rms(y) * g_ref[2:3, :])


def _sgu_s5_mix_out(r, x_ref, mod_ref, g_ref, mix_refs, z_scr):
    ug_ref, vn_ref, yf_ref, yb_ref, us_ref, sw_ref, sb_ref, d_ref, gw_ref, gb_ref, wo_ref = mix_refs
    d_a = ug_ref.shape[1]
    row_chunks = range(r.start, r.stop, SGU_CHUNK)
    for g in range(d_a // SGU_CHUNK):
        c0 = g * SGU_CHUNK
        vn_wide = jnp.concatenate([vn_ref[r0:r0 + SGU_CHUNK, c0:c0 + SGU_CHUNK] for r0 in row_chunks], axis=1)
        mixed_wide = _mm(sw_ref[g], vn_wide)
        for n, r0 in enumerate(row_chunks):
            mixed = mixed_wide[:, n * SGU_CHUNK:(n + 1) * SGU_CHUNK] + sb_ref[g]
            z_scr[r0:r0 + SGU_CHUNK, c0:c0 + SGU_CHUNK] = (
                ug_ref[r0:r0 + SGU_CHUNK, c0:c0 + SGU_CHUNK].astype(F32) * mixed).astype(BF16)
    ys = jax.nn.gelu(yf_ref[r, :] + yb_ref[r, :] + d_ref[...] * us_ref[r, :])
    gl = jax.nn.sigmoid(_mm(ys.astype(BF16), gw_ref[...]) + gb_ref[...])
    z_scr[r, d_a:] = (ys * gl).astype(BF16)
    y = _mm(z_scr[r, :], wo_ref[...])
    return x_ref[r, :] + mod_ref[5:6, :] * (_rms(y) * g_ref[2:3, :])


_MIX_OUT = {"attn": (_attn_mix_out, 2, False), "sgu_s5": (_sgu_s5_mix_out, 11, True)}


def _ffn_kernel(sub, n_sub, mixer, x_ref, mod_ref, g_ref, win_ref, wout_ref, *rest):
    mix_fn, n_mix, needs_z = _MIX_OUT[mixer] if mixer else (None, 0, False)
    mix_refs, rest = rest[:n_mix], rest[n_mix:]
    o_ref, h_scr, a_scr, y_scr = rest[:4]
    z_scr = rest[4] if needs_z else None
    d_ff = wout_ref.shape[0]
    rows_per = x_ref.shape[0] // n_sub
    shift, scale, gate = (mod_ref[3 * sub + k:3 * sub + k + 1, :] for k in range(3))
    pre_gain = g_ref[0:1, :] * (1.0 + scale)
    post_gain = (FFN_RES * gate) * g_ref[1:2, :]

    def rows(s):
        return slice(s * rows_per, (s + 1) * rows_per)

    def prologue(s):
        if mix_fn is None:
            x = x_ref[rows(s), :]
        else:
            x = mix_fn(rows(s), x_ref, mod_ref, g_ref, mix_refs, z_scr)
            o_ref[rows(s), :] = x
        h_scr[rows(s), :] = (_rms(x) * pre_gain + shift).astype(BF16)

    def up_chunk(s, c0, c1):
        h = h_scr[rows(s), :]
        g = _mm(h, win_ref[:, c0:c1])
        u = _mm(h, win_ref[:, d_ff + c0:d_ff + c1])
        a_scr[rows(s), c0:c1] = (g * jax.nn.sigmoid(g) * u).astype(BF16)

    def epilogue(s):
        base = x_ref if mix_fn is None else o_ref
        o_ref[rows(s), :] = base[rows(s), :] + post_gain * _rms(y_scr[rows(s), :])

    prologue(0)
    for s in range(n_sub):
        for ci, (c0, c1) in enumerate(_ffn_chunks(d_ff)):
            up_chunk(s, c0, c1)
            if ci == 0:
                if s > 0:
                    epilogue(s - 1)
                if s + 1 < n_sub:
                    prologue(s + 1)
        y_scr[rows(s), :] = _mm(a_scr[rows(s), :], wout_ref[...])
    epilogue(n_sub - 1)


def _ffn(x, mod, gains, w_in, w_out, sub, row_of, mixer=None, mix_rows=(), mix_consts=(), tile=None):
    r, d = x.shape
    d_ff = w_out.shape[0]
    tm, n_sub = tile or _tiles()["ffn_mix" if mixer else "ffn"]

    def row_spec(a):
        if isinstance(a, tuple):
            return a[1](tm)
        return pl.BlockSpec((tm, a.shape[1]), lambda i: (i, 0))

    mix_arrays = [a[0] if isinstance(a, tuple) else a for a in mix_rows]
    scratch = [pltpu.VMEM((tm, d), BF16), pltpu.VMEM((tm, d_ff), BF16), pltpu.VMEM((tm, d), F32)]
    if mixer and _MIX_OUT[mixer][2]:
        scratch.append(pltpu.VMEM((tm, d), BF16))
    return pl.pallas_call(
        functools.partial(_ffn_kernel, sub, n_sub, mixer),
        grid=(r // tm,),
        in_specs=[row_spec(x),
                  pl.BlockSpec((None, N_MOD_ROWS, d), lambda i: (row_of(i, tm), 0, 0)),
                  _const_spec(gains.shape),
                  _resident_spec(w_in.shape),
                  _resident_spec(w_out.shape)]
                 + [row_spec(a) for a in mix_rows] + [_resident_spec(c.shape) for c in mix_consts],
        out_specs=row_spec(x),
        out_shape=jax.ShapeDtypeStruct((r, d), F32),
        scratch_shapes=scratch,
        compiler_params=_params(("parallel",)),
        name=f"ffn{sub}" + (f"_{mixer}" if mixer else ""),
    )(x, mod, gains, w_in, w_out, *mix_arrays, *mix_consts)


def _mix0_in_kernel(x_ref, mod_ref, g_ref, w_ref, ng_ref, ug_ref, vn_ref, us_ref):
    d_a = ug_ref.shape[1]
    x = x_ref[...]
    shift, scale = mod_ref[3:4, :], mod_ref[4:5, :]
    h = (_rms(x) * g_ref[0:1, :] * (1.0 + scale) + shift).astype(BF16)
    ug_ref[...] = jax.nn.gelu(_mm(h, w_ref[:, 0:d_a])).astype(BF16)
    for c0 in range(0, d_a, MXU_COLS):
        vv = jax.nn.gelu(_mm(h, w_ref[:, d_a + c0:d_a + c0 + MXU_COLS]))
        for g0 in range(0, MXU_COLS, SGU_CHUNK):
            v = vv[:, g0:g0 + SGU_CHUNK]
            mu = jnp.mean(v, axis=-1, keepdims=True)
            vc = v - mu
            var = jnp.mean(vc * vc, axis=-1, keepdims=True)
            vn_ref[:, c0 + g0:c0 + g0 + SGU_CHUNK] = (
                vc * lax.rsqrt(var + NORM_EPS) * ng_ref[:, c0 + g0:c0 + g0 + SGU_CHUNK]).astype(BF16)
    us_ref[...] = _mm(h, w_ref[:, 2 * d_a:])


def _mix0_in_lat_kernel(x_ref, mod_ref, g_ref, w_ref, ng_ref, us_in, ug_ref, vn_ref, us_ref):
    del us_in
    _mix0_in_kernel(x_ref, mod_ref, g_ref, w_ref, ng_ref, ug_ref, vn_ref, us_ref)


def _mix0_in(xl, xc, mod, gains, w_in, norm_g, bsz):
    d = xl.shape[1]
    d_a = norm_g.shape[1]
    d_b = w_in.shape[1] - 2 * d_a
    seq, n_ctx = xl.shape[0] // bsz, xc.shape[0] // bsz
    tm = _tiles()["mix"]
    blocks = seq // tm
    consts = [_const_spec(gains.shape), _const_spec(w_in.shape), _const_spec(norm_g.shape)]
    us_shape = jax.ShapeDtypeStruct((bsz, seq + n_ctx, d_b), F32)

    def act_shapes(r):
        return [jax.ShapeDtypeStruct((r, d_a), BF16)] * 2

    ug_c, vn_c, us_all = pl.pallas_call(
        _mix0_in_kernel,
        grid=(bsz,),
        in_specs=[pl.BlockSpec((n_ctx, d), lambda b: (b, 0)),
                  pl.BlockSpec((None, N_MOD_ROWS, d), lambda b: (CTX_MOD_ROW, 0, 0))] + consts,
        out_specs=[pl.BlockSpec((n_ctx, d_a), lambda b: (b, 0))] * 2
                  + [pl.BlockSpec((None, n_ctx, d_b), lambda b: (b, seq // n_ctx, 0))],
        out_shape=act_shapes(bsz * n_ctx) + [us_shape],
        compiler_params=_params(("parallel",)),
        name="mix0_in_ctx",
    )(xc, mod, gains, w_in, norm_g)
    ug_l, vn_l, us_all = pl.pallas_call(
        _mix0_in_lat_kernel,
        grid=(bsz * blocks,),
        in_specs=[pl.BlockSpec((tm, d), lambda i: (i, 0)),
                  pl.BlockSpec((None, N_MOD_ROWS, d), lambda i: (i // blocks, 0, 0))] + consts
                 + [pl.BlockSpec(memory_space=pl.ANY)],
        out_specs=[pl.BlockSpec((tm, d_a), lambda i: (i, 0))] * 2
                  + [pl.BlockSpec((None, tm, d_b), lambda i: (i // blocks, i % blocks, 0))],
        out_shape=act_shapes(bsz * seq) + [us_shape],
        input_output_aliases={5: 2},
        compiler_params=_params(("parallel",)),
        name="mix0_in",
    )(xl, mod, gains, w_in, norm_g, us_all)
    return (ug_l, vn_l), (ug_c, vn_c), us_all


def _s5_kernel(steps, uf_ref, ub_ref, bf_ref, bb_ref, cf_ref, cb_ref, lam_ref,
               yf_ref, yb_ref, sf_scr, sb_scr, hf_scr, hb_scr, r_scr, t_scr):
    nb = V7X_SUBLANES
    n = lam_ref.shape[1]
    lane_slabs = [slice(c, c + V7X_LANES) for c in range(0, uf_ref.shape[2], V7X_LANES)]
    pitch = r_scr.shape[2] // nb

    def batch_rows(b):
        return slice(b * pitch, b * pitch + steps)

    def time_rows(t):
        return slice(t * nb, (t + 1) * nb)

    def to_time_major(use, u_ref):
        for b in range(nb):
            for sl, lanes in enumerate(lane_slabs):
                r_scr[use, sl, batch_rows(b), :] = u_ref[b, :, lanes]
        for t in range(steps):
            for sl, lanes in enumerate(lane_slabs):
                t_scr[use, time_rows(t), lanes] = r_scr[use, sl, pl.ds(t, nb, stride=pitch), :]
        return t_scr[use]

    def from_time_major(use, y, y_ref):
        t_scr[use] = y
        for t in range(steps):
            for sl, lanes in enumerate(lane_slabs):
                r_scr[use, sl, pl.ds(t, nb, stride=pitch), :] = t_scr[use, time_rows(t), lanes]
        for b in range(nb):
            for sl, lanes in enumerate(lane_slabs):
                y_ref[b, :, lanes] = r_scr[use, sl, batch_rows(b), :]

    @pl.when(pl.program_id(0) == 0)
    def _():
        hf_scr[...] = jnp.zeros_like(hf_scr)
        hb_scr[...] = jnp.zeros_like(hb_scr)

    sf_scr[...] = _mm(to_time_major(0, uf_ref).astype(BF16), bf_ref[...])
    sb_scr[...] = _mm(to_time_major(1, ub_ref).astype(BF16), bb_ref[...])

    def scan(s_scr, h_scr, lam_row, reverse):
        lr = lam_ref[lam_row:lam_row + nb, :]
        li = lam_ref[lam_row + nb:lam_row + 2 * nb, :]
        hr, hi = h_scr[:, 0:n], h_scr[:, n:2 * n]
        for k in range(steps):
            t = (steps - 1 - k) if reverse else k
            rows = slice(t * nb, (t + 1) * nb)
            hr, hi = (lr * hr - li * hi + s_scr[rows, 0:n],
                      lr * hi + li * hr + s_scr[rows, n:2 * n])
            s_scr[rows, 0:n] = hr
            s_scr[rows, n:2 * n] = hi
        h_scr[:, 0:n] = hr
        h_scr[:, n:2 * n] = hi

    scan(sf_scr, hf_scr, 0, False)
    from_time_major(2, _mm(sf_scr[...].astype(BF16), cf_ref[...]), yf_ref)
    scan(sb_scr, hb_scr, 2 * nb, True)
    from_time_major(3, _mm(sb_scr[...].astype(BF16), cb_ref[...]), yb_ref)


def _s5_scan(us_all, seq, b_mats, c_mats, lam):
    bsz, t_all, d_b = us_all.shape
    nb = V7X_SUBLANES
    steps = _tiles()["s5_steps"]
    n_lat, n_chunks = seq // steps, t_all // steps
    n_ctx = n_chunks - n_lat
    n2 = b_mats.shape[-1]
    pitch = steps + nb
    n_slabs = d_b // V7X_LANES

    def fwd_chunk(k):
        return jnp.where(k < n_ctx, n_lat + k, k - n_ctx)

    def bwd_chunk(k):
        return jnp.where(k < n_ctx, n_chunks - 1 - k, n_lat - 1 - (k - n_ctx))

    blk = (bsz, steps, d_b)
    return pl.pallas_call(
        functools.partial(_s5_kernel, steps),
        grid=(n_chunks,),
        in_specs=[pl.BlockSpec(blk, lambda k: (0, fwd_chunk(k), 0)),
                  pl.BlockSpec(blk, lambda k: (0, bwd_chunk(k), 0)),
                  pl.BlockSpec((None, d_b, n2), lambda k: (0, 0, 0)),
                  pl.BlockSpec((None, d_b, n2), lambda k: (1, 0, 0)),
                  pl.BlockSpec((None, n2, d_b), lambda k: (0, 0, 0)),
                  pl.BlockSpec((None, n2, d_b), lambda k: (1, 0, 0)),
                  _const_spec(lam.shape)],
        out_specs=[pl.BlockSpec(blk, lambda k: (0, fwd_chunk(k), 0)),
                   pl.BlockSpec(blk, lambda k: (0, bwd_chunk(k), 0))],
        out_shape=[jax.ShapeDtypeStruct(us_all.shape, F32)] * 2,
        scratch_shapes=[pltpu.VMEM((steps * nb, n2), F32), pltpu.VMEM((steps * nb, n2), F32),
                        pltpu.VMEM((nb, n2), F32), pltpu.VMEM((nb, n2), F32),
                        pltpu.VMEM((4, n_slabs, nb * pitch, V7X_LANES), F32),
                        pltpu.VMEM((4, steps * nb, d_b), F32)],
        compiler_params=_params(("arbitrary",)),
        name="s5_scan",
    )(us_all, us_all, b_mats, b_mats, c_mats, c_mats, lam)


def _s5_operands(lam_re, lam_im, log_step, b_re, b_im, c_re, c_im):
    n_dir, groups, states = lam_re.shape
    gdim = b_re.shape[-1]
    dt = jnp.exp(log_step.astype(F32))[..., None]
    lr, li = lam_re.astype(F32), lam_im.astype(F32)
    mag = jnp.exp(lr * dt)
    ar, ai = mag * jnp.cos(li * dt), mag * jnp.sin(li * dt)
    den = lr * lr + li * li
    fr = ((ar - 1.0) * lr + ai * li) / den
    fi = (ai * lr - (ar - 1.0) * li) / den
    bbr = fr[..., None] * b_re - fi[..., None] * b_im
    bbi = fr[..., None] * b_im + fi[..., None] * b_re
    eye = jnp.eye(groups, dtype=F32)
    n = groups * states

    def in_mat(b):
        return jnp.einsum("dgpc,gh->dgchp", b, eye).reshape(n_dir, groups * gdim, n)

    def out_mat(c):
        return jnp.einsum("dgcp,gh->dgphc", c, eye).reshape(n_dir, n, groups * gdim)

    b_mats = jnp.concatenate([in_mat(bbr), in_mat(bbi)], axis=-1).astype(BF16)
    c_mats = jnp.concatenate([out_mat(c_re.astype(F32)), -out_mat(c_im.astype(F32))], axis=1).astype(BF16)
    nb = V7X_SUBLANES
    lam = jnp.concatenate([jnp.broadcast_to(v.reshape(1, n), (nb, n))
                           for v in (ar[0], ai[0], ar[1], ai[1])], axis=0)
    return b_mats, c_mats, lam


def _rope(x, cos, sin_lo, sin_hi):
    quarter = HEAD_DIM // 4
    return (x * cos + pltpu.roll(x, HEAD_DIM - quarter, axis=1) * sin_lo
            + pltpu.roll(x, quarter, axis=1) * sin_hi)


def _modulated(x_ref, mod_ref, g_ref):
    return (_rms(x_ref[...]) * g_ref[0:1, :] * (1.0 + mod_ref[4:5, :]) + mod_ref[3:4, :]).astype(BF16)


def _head_slices(h, w_ref, col0, n_heads):
    per_dot = MXU_COLS // HEAD_DIM
    assert n_heads % per_dot == 0
    for pair in range(n_heads // per_dot):
        c0 = col0 + pair * MXU_COLS
        wide = _mm(h, w_ref[:, c0:c0 + MXU_COLS])
        for sub in range(per_dot):
            yield wide[:, sub * HEAD_DIM:(sub + 1) * HEAD_DIM]


def _kv_heads(h, w_ref, ng_ref, q_dim, k_ref, vt_ref, rope):
    kvh = vt_ref.shape[0]
    tokens = vt_ref.shape[2]
    for hd, k in enumerate(_head_slices(h, w_ref, q_dim, kvh)):
        k = _rms(k) * ng_ref[1:2, :]
        if rope is not None:
            k = _rope(k, *rope)
        k_ref[:, hd * HEAD_DIM:(hd + 1) * HEAD_DIM] = k.astype(BF16)
    for hd, v in enumerate(_head_slices(h, w_ref, q_dim + kvh * HEAD_DIM, kvh)):
        vt_ref[hd, 0:HEAD_DIM, :] = v.T.astype(BF16)
        vt_ref[hd, HEAD_DIM:, :] = jnp.ones((V_ONES_ROWS, tokens), BF16)


def _kv_ctx_kernel(x_ref, mod_ref, g_ref, w_ref, ng_ref, k_ref, vt_ref):
    q_dim = w_ref.shape[1] - 2 * k_ref.shape[1]
    _kv_heads(_modulated(x_ref, mod_ref, g_ref), w_ref, ng_ref, q_dim, k_ref, vt_ref, None)


def _qkv_lat_kernel(x_ref, mod_ref, g_ref, w_ref, ng_ref, rope_ref, k_in, vt_in, qt_ref, k_ref, vt_ref):
    del k_in, vt_in
    heads = qt_ref.shape[0]
    h = _modulated(x_ref, mod_ref, g_ref)
    rope = (rope_ref[0], rope_ref[1], rope_ref[2])
    q_gain = ng_ref[0:1, :] * (HEAD_DIM ** -0.5 * math.log2(math.e))
    for hd, q in enumerate(_head_slices(h, w_ref, 0, heads)):
        qt_ref[hd] = _rope(_rms(q) * q_gain, *rope).T.astype(BF16)
    _kv_heads(h, w_ref, ng_ref, heads * HEAD_DIM, k_ref, vt_ref, rope)


def _qkv(xl, xc, mod, gains, w_qkv, qk_gains, rope_tab, bsz):
    d = xl.shape[1]
    seq, n_ctx = xl.shape[0] // bsz, xc.shape[0] // bsz
    kv_dim = (w_qkv.shape[1] - d) // 2
    heads, kvh = d // HEAD_DIM, kv_dim // HEAD_DIM
    l_all = seq + n_ctx
    tm = _tiles()["mix"]
    pos_blocks = seq // tm
    ctx_blk = seq // n_ctx
    kv_shapes = [jax.ShapeDtypeStruct((bsz, l_all, kv_dim), BF16),
                 jax.ShapeDtypeStruct((bsz, kvh, V_ROWS, l_all), BF16)]
    consts = [_const_spec(gains.shape), _const_spec(w_qkv.shape), _const_spec(qk_gains.shape)]
    k_all, vt_all = pl.pallas_call(
        _kv_ctx_kernel,
        grid=(bsz,),
        in_specs=[pl.BlockSpec((n_ctx, d), lambda b: (b, 0)),
                  pl.BlockSpec((None, N_MOD_ROWS, d), lambda b: (CTX_MOD_ROW, 0, 0))] + consts,
        out_specs=[pl.BlockSpec((None, n_ctx, kv_dim), lambda b: (b, ctx_blk, 0)),
                   pl.BlockSpec((None, kvh, V_ROWS, n_ctx), lambda b: (b, 0, 0, ctx_blk))],
        out_shape=kv_shapes,
        compiler_params=_params(("parallel",)),
        name="kv_ctx",
    )(xc, mod, gains, w_qkv, qk_gains)
    any_spec = pl.BlockSpec(memory_space=pl.ANY)
    return pl.pallas_call(
        _qkv_lat_kernel,
        grid=(bsz * pos_blocks,),
        in_specs=[pl.BlockSpec((tm, d), lambda i: (i, 0)),
                  pl.BlockSpec((None, N_MOD_ROWS, d), lambda i: (i // pos_blocks, 0, 0))] + consts
                 + [pl.BlockSpec((3, tm, HEAD_DIM), lambda i: (0, i % pos_blocks, 0)), any_spec, any_spec],
        out_specs=[pl.BlockSpec((None, heads, HEAD_DIM, tm), lambda i: (i // pos_blocks, 0, 0, i % pos_blocks)),
                   pl.BlockSpec((None, tm, kv_dim), lambda i: (i // pos_blocks, i % pos_blocks, 0)),
                   pl.BlockSpec((None, kvh, V_ROWS, tm), lambda i: (i // pos_blocks, 0, 0, i % pos_blocks))],
        out_shape=[jax.ShapeDtypeStruct((bsz, heads, HEAD_DIM, seq), BF16)] + kv_shapes,
        input_output_aliases={6: 1, 7: 2},
        compiler_params=_params(("parallel",)),
        name="qkv",
    )(xl, mod, gains, w_qkv, qk_gains, rope_tab, k_all, vt_all)


def _rope_tables(seq):
    rows = seq // GRID_W
    axis_dim = HEAD_DIM // 2
    quarter = axis_dim // 2
    row_id = jnp.repeat(jnp.arange(rows, dtype=F32), GRID_W)
    col_id = jnp.tile(jnp.arange(GRID_W, dtype=F32), rows)
    inv_freq = ROPE_THETA ** (-jnp.arange(0, axis_dim, 2, dtype=F32) / axis_dim)
    a_row, a_col = row_id[:, None] * inv_freq, col_id[:, None] * inv_freq
    zero = jnp.zeros((seq, quarter), F32)
    cos = jnp.concatenate([jnp.cos(a_row)] * 2 + [jnp.cos(a_col)] * 2, axis=1)
    sin_lo = jnp.concatenate([-jnp.sin(a_row), zero, -jnp.sin(a_col), zero], axis=1)
    sin_hi = jnp.concatenate([zero, jnp.sin(a_row), zero, jnp.sin(a_col)], axis=1)
    return jnp.stack([cos, sin_lo, sin_hi])


def _attn_kernel(qt_ref, qn_ref, k_ref, kn_ref, vt_ref, o_ref, s_scr, m_scr):
    heads = qt_ref.shape[0]
    n_keys = k_ref.shape[0]
    chunks = [(c, min(c + ATTN_KEY_CHUNK, n_keys)) for c in range(0, n_keys, ATTN_KEY_CHUNK)]

    def scores(q, keys_ref, slot, c0, c1, m):
        s = _mm(keys_ref[c0:c1, :], q)
        s_scr[slot, c0:c1, :] = s
        cm = jnp.max(s, axis=0, keepdims=True)
        return cm if m is None else jnp.maximum(m, cm)

    @pl.when((pl.program_id(0) == 0) & (pl.program_id(1) == 0) & (pl.program_id(2) == 0))
    def _():
        m0 = None
        for c0, c1 in chunks:
            m0 = scores(qt_ref[0], k_ref, 0, c0, c1, m0)
        m_scr[...] = m0

    m = m_scr[...]
    for h in range(heads):
        last = h + 1 == heads
        q_next, keys_next = (qn_ref[...], kn_ref) if last else (qt_ref[h + 1], k_ref)
        m_next, acc = None, None
        for c0, c1 in chunks:
            m_next = scores(q_next, keys_next, (h + 1) % 2, c0, c1, m_next)
            p = jnp.exp2(s_scr[h % 2, c0:c1, :] - m).astype(BF16)
            pv = _mm(vt_ref[:, c0:c1], p)
            acc = pv if acc is None else acc + pv
        o = acc[0:HEAD_DIM, :] / acc[HEAD_DIM:HEAD_DIM + 1, :]
        o_ref[:, h * HEAD_DIM:(h + 1) * HEAD_DIM] = o.T.astype(BF16)
        m = m_next
    m_scr[...] = m


def _attention(qt, k, vt):
    bsz, heads, _, t = qt.shape
    l, kvh = k.shape[1], vt.shape[1]
    q_per_kv = heads // kvh
    assert q_per_kv % 2 == 0, "score buffers alternate per head and must line up across grid steps"
    tq = _tiles()["attn_q"]
    nq = t // tq
    steps = bsz * kvh * nq

    def following(b, j, i):
        lin = jnp.minimum((b * kvh + j) * nq + i + 1, steps - 1)
        return lin // (kvh * nq), (lin // nq) % kvh, lin % nq

    def q_next_map(b, j, i):
        b2, j2, i2 = following(b, j, i)
        return b2, j2 * q_per_kv, 0, i2

    def k_next_map(b, j, i):
        b2, j2, _ = following(b, j, i)
        return b2, 0, j2

    return pl.pallas_call(
        _attn_kernel,
        grid=(bsz, kvh, nq),
        in_specs=[pl.BlockSpec((None, q_per_kv, HEAD_DIM, tq), lambda b, j, i: (b, j, 0, i)),
                  pl.BlockSpec((None, None, HEAD_DIM, tq), q_next_map),
                  pl.BlockSpec((None, l, HEAD_DIM), lambda b, j, i: (b, 0, j)),
                  pl.BlockSpec((None, l, HEAD_DIM), k_next_map),
                  pl.BlockSpec((None, None, V_ROWS, l), lambda b, j, i: (b, j, 0, 0))],
        out_specs=pl.BlockSpec((None, tq, q_per_kv * HEAD_DIM), lambda b, j, i: (b, i, j)),
        out_shape=jax.ShapeDtypeStruct((bsz, t, heads * HEAD_DIM), BF16),
        scratch_shapes=[pltpu.VMEM((2, l, tq), F32), pltpu.VMEM((1, tq), F32)],
        compiler_params=_params(("arbitrary",) * 3),
        name="attention",
    )(qt, qt, k, k, vt)


def kernel(x, c, ctx, c_ctx, w_mod, b_mod, norm_pre, norm_post, ffn_w_in, ffn_w_out, ab_w_in, ab_w_out, sgu_norm_g, sgu_w, sgu_b, s5_lam_re, s5_lam_im, s5_log_step, s5_b_re, s5_b_im, s5_c_re, s5_c_im, s5_d, s5_glu_w, s5_glu_b, attn_w_qkv, attn_w_out, attn_q_norm, attn_k_norm):
    bsz, seq, d = x.shape
    n_ctx = ctx.shape[1]
    depth = w_mod.shape[0]
    assert bsz == V7X_SUBLANES, "the S5 scan keeps the batch on the sublane axis"
    d_a = sgu_norm_g.shape[1]
    d_b = s5_d.shape[1]
    kv_dim = (attn_w_qkv.shape[2] - d) // 2
    q_per_kv = d // kv_dim

    def lat_row(i, tm):
        return (i * tm) // seq

    def ctx_row(i, tm):
        return CTX_MOD_ROW

    cond = jnp.zeros((N_MOD_ROWS, d), F32).at[:bsz].set(c).at[CTX_MOD_ROW].set(c_ctx)
    mod_all = _modulation(cond, w_mod, b_mod).reshape(depth, N_MOD_ROWS, -1, d)
    mod_all = jnp.pad(mod_all, ((0, 0), (0, 0), (0, N_MOD_ROWS - mod_all.shape[2]), (0, 0)))

    xl = x.reshape(bsz * seq, d)
    xc = ctx.reshape(bsz * n_ctx, d)
    streams = ((lat_row, seq), (ctx_row, n_ctx))

    for i in range(depth):
        last = i == depth - 1
        j = i // 2
        mod = mod_all[i]
        gains = [jnp.zeros((V7X_SUBLANES, d), F32).at[0].set(norm_pre[i, s]).at[1].set(norm_post[i, s])
                 for s in range(3)]
        w_in1, w_in2 = ffn_w_in[i, 0].astype(BF16), ffn_w_in[i, 1].astype(BF16)
        w_out1, w_out2 = ffn_w_out[i, 0].astype(BF16), ffn_w_out[i, 1].astype(BF16)

        xl = _ffn(xl, mod, gains[0], w_in1, w_out1, 0, lat_row)
        xc = _ffn(xc, mod, gains[0], w_in1, w_out1, 0, ctx_row)

        if i % 2 == 0:
            w_in = ab_w_in[j].astype(BF16)
            norm_g = sgu_norm_g[j].reshape(1, d_a)
            acts_l, acts_c, us_all = _mix0_in(xl, xc, mod, gains[1], w_in, norm_g, bsz)
            b_mats, c_mats, lam = _s5_operands(s5_lam_re[j], s5_lam_im[j], s5_log_step[j], s5_b_re[j],
                                               s5_b_im[j], s5_c_re[j], s5_c_im[j])
            yf, yb = _s5_scan(us_all, seq, b_mats, c_mats, lam)

            def lat_s5_spec(tm):
                return pl.BlockSpec((None, tm, d_b), lambda i: ((i * tm) // seq, i % (seq // tm), 0))

            def ctx_s5_spec(tm):
                assert tm == n_ctx
                return pl.BlockSpec((None, n_ctx, d_b), lambda i: (i, seq // n_ctx, 0))

            consts = (sgu_w[j].astype(BF16),
                      jnp.broadcast_to(sgu_b[j][:, :, None], sgu_w[j].shape).astype(F32),
                      s5_d[j].reshape(1, d_b), s5_glu_w[j].astype(BF16), s5_glu_b[j].reshape(1, d_b),
                      ab_w_out[j].astype(BF16))
            mixer = "sgu_s5"
            mix_l = acts_l + tuple((a, lat_s5_spec) for a in (yf, yb, us_all))
            mix_c = acts_c + tuple((a, ctx_s5_spec) for a in (yf, yb, us_all))
            ctx_tile = (n_ctx, 2)
        else:
            if not last:
                raise NotImplementedError("context stream through an attention layer")
            w_qkv = attn_w_qkv[j].astype(BF16)
            qk_gains = jnp.zeros((V7X_SUBLANES, HEAD_DIM), F32).at[0].set(attn_q_norm[j]).at[1].set(attn_k_norm[j])
            rope_tab = _rope_tables(seq)
            qt, k, vt = _qkv(xl, xc, mod, gains[1], w_qkv, qk_gains, rope_tab, bsz)
            mixer = "attn"
            mix_l, mix_c = (_attention(qt, k, vt).reshape(bsz * seq, d),), None
            consts = (attn_w_out[j].astype(BF16),)
            ctx_tile = None

        gains2 = gains[2].at[2].set(norm_post[i, 1])
        xl = _ffn(xl, mod, gains2, w_in2, w_out2, 2, lat_row, mixer, mix_l, consts)
        if not last:
            xc = _ffn(xc, mod, gains2, w_in2, w_out2, 2, ctx_row, mixer, mix_c, consts, ctx_tile)

    return xl.reshape(bsz, seq, d)
```

```python
import functools
import math

import jax
import jax.numpy as jnp
from jax import lax
from jax.experimental import pallas as pl
from jax.experimental.pallas import tpu as pltpu

F32 = jnp.float32
BF16 = jnp.bfloat16

NORM_EPS = 1e-6
FFN_RES = 0.5
GRID_W = 64
ROPE_THETA = 10000.0
SGU_CHUNK = 128
HEAD_DIM = 128
ATTN_KEY_CHUNK = 512
V_ONES_ROWS = 16
V_ROWS = HEAD_DIM + V_ONES_ROWS

V7X_LANES = 128
V7X_SUBLANES = 8
MXU_COLS = 256
V7X_VMEM_BYTES = 64 * 1024 * 1024
VMEM_LIMIT = 56 * 1024 * 1024

N_MOD_ROWS = 16
CTX_MOD_ROW = 8


def _tiles():
    return dict(ffn=(1024, 4), ffn_mix=(512, 2), mix=512, mix_sub=2, attn_q=512, s5_steps=64, mod_n=2304)


def _params(sem, vmem=VMEM_LIMIT):
    return pltpu.CompilerParams(dimension_semantics=sem, vmem_limit_bytes=vmem)


def _const_spec(shape):
    nd = len(shape)
    return pl.BlockSpec(shape, lambda *_: (0,) * nd)


def _resident_spec(shape, lead=()):
    nd = len(shape) - len(lead)
    return pl.BlockSpec((None,) * len(lead) + tuple(shape[len(lead):]),
                        lambda *_: tuple(lead) + (0,) * nd, pipeline_mode=pl.Buffered(1))


def _rms(x):
    return x * lax.rsqrt(jnp.mean(x * x, axis=-1, keepdims=True) + NORM_EPS)


def _mm(a, b):
    return jnp.dot(a, b, preferred_element_type=F32)


def _mod_kernel(c_ref, w_ref, b_ref, o_ref):
    c = c_ref[...]
    s = (c * jax.nn.sigmoid(c)).astype(BF16)
    o_ref[...] = _mm(s, w_ref[...].astype(BF16)) + b_ref[...]


def _modulation(cond, w_mod, b_mod):
    depth, d, n = w_mod.shape
    tn = _tiles()["mod_n"]
    return pl.pallas_call(
        _mod_kernel,
        grid=(depth, n // tn),
        in_specs=[_const_spec((N_MOD_ROWS, d)),
                  pl.BlockSpec((None, d, tn), lambda l, j: (l, 0, j)),
                  pl.BlockSpec((None, 1, tn), lambda l, j: (l, 0, j))],
        out_specs=pl.BlockSpec((None, N_MOD_ROWS, tn), lambda l, j: (l, 0, j)),
        out_shape=jax.ShapeDtypeStruct((depth, N_MOD_ROWS, n), F32),
        compiler_params=_params(("parallel", "parallel")),
        name="modulation",
    )(cond, w_mod, b_mod.reshape(depth, 1, n))


def _ffn_chunks(d_ff, width=512):
    return [(c, min(c + width, d_ff)) for c in range(0, d_ff, width)]


def _attn_mix_out(r, x_ref, mod_ref, g_ref, mix_refs, z_scr):
    a_ref, wo_ref = mix_refs
    y = _mm(a_ref[r, :], wo_ref[...])
    return x_ref[r, :] + mod_ref[5:6, :] * (_rms(y) * g_ref[2:3, :])


def _sgu_s5_mix_out(r, x_ref, mod_ref, g_ref, mix_refs, z_scr):
    ug_ref, vn_ref, yf_ref, yb_ref, us_ref, sw_ref, sb_ref, d_ref, gw_ref, gb_ref, wo_ref = mix_refs
    d_a = ug_ref.shape[1]
    row_chunks = range(r.start, r.stop, SGU_CHUNK)
    for g in range(d_a // SGU_CHUNK):
        c0 = g * SGU_CHUNK
        vn_wide = jnp.concatenate([vn_ref[r0:r0 + SGU_CHUNK, c0:c0 + SGU_CHUNK] for r0 in row_chunks], axis=1)
        mixed_wide = _mm(sw_ref[g], vn_wide)
        for n, r0 in enumerate(row_chunks):
            mixed = mixed_wide[:, n * SGU_CHUNK:(n + 1) * SGU_CHUNK] + sb_ref[g]
            z_scr[r0:r0 + SGU_CHUNK, c0:c0 + SGU_CHUNK] = (
                ug_ref[r0:r0 + SGU_CHUNK, c0:c0 + SGU_CHUNK].astype(F32) * mixed).astype(BF16)
    ys = jax.nn.gelu(yf_ref[r, :] + yb_ref[r, :] + d_ref[...] * us_ref[r, :])
    gl = jax.nn.sigmoid(_mm(ys.astype(BF16), gw_ref[...]) + gb_ref[...])
    z_scr[r, d_a:] = (ys * gl).astype(BF16)
    y = _mm(z_scr[r, :], wo_ref[...])
    return x_ref[r, :] + mod_ref[5:6, :] * (_rms(y) * g_ref[2:3, :])


_MIX_OUT = {"attn": (_attn_mix_out, 2, False), "sgu_s5": (_sgu_s5_mix_out, 11, True)}


def _ffn_kernel(sub, n_sub, mixer, x_ref, mod_ref, g_ref, win_ref, wout_ref, *rest):
    mix_fn, n_mix, needs_z = _MIX_OUT[mixer] if mixer else (None, 0, False)
    mix_refs, rest = rest[:n_mix], rest[n_mix:]
    o_ref, h_scr, a_scr, y_scr = rest[:4]
    z_scr = rest[4] if needs_z else None
    d_ff = wout_ref.shape[0]
    rows_per = x_ref.shape[0] // n_sub
    shift, scale, gate = (mod_ref[3 * sub + k:3 * sub + k + 1, :] for k in range(3))
    pre_gain = g_ref[0:1, :] * (1.0 + scale)
    post_gain = (FFN_RES * gate) * g_ref[1:2, :]

    def rows(s):
        return slice(s * rows_per, (s + 1) * rows_per)

    def prologue(s):
        if mix_fn is None:
            x = x_ref[rows(s), :]
        else:
            x = mix_fn(rows(s), x_ref, mod_ref, g_ref, mix_refs, z_scr)
            o_ref[rows(s), :] = x
        h_scr[rows(s), :] = (_rms(x) * pre_gain + shift).astype(BF16)

    def up_chunk(s, c0, c1):
        h = h_scr[rows(s), :]
        g = _mm(h, win_ref[:, c0:c1])
        u = _mm(h, win_ref[:, d_ff + c0:d_ff + c1])
        a_scr[rows(s), c0:c1] = (g * jax.nn.sigmoid(g) * u).astype(BF16)

    def epilogue(s):
        base = x_ref if mix_fn is None else o_ref
        o_ref[rows(s), :] = base[rows(s), :] + post_gain * _rms(y_scr[rows(s), :])

    prologue(0)
    for s in range(n_sub):
        for ci, (c0, c1) in enumerate(_ffn_chunks(d_ff)):
            up_chunk(s, c0, c1)
            if ci == 0:
                if s > 0:
                    epilogue(s - 1)
                if s + 1 < n_sub:
                    prologue(s + 1)
        y_scr[rows(s), :] = _mm(a_scr[rows(s), :], wout_ref[...])
    epilogue(n_sub - 1)


def _ffn(x, mod, gains, w_in, w_out, sub, row_of, mixer=None, mix_rows=(), mix_consts=(), tile=None):
    r, d = x.shape
    (w_in, in_lead), (w_out, out_lead) = w_in, w_out
    d_ff = w_out.shape[-2]
    tm, n_sub = tile or _tiles()["ffn_mix" if mixer else "ffn"]

    row_spec = lambda a: pl.BlockSpec((tm, a.shape[1]), lambda i: (i, 0))
    scratch = [pltpu.VMEM((tm, d), BF16), pltpu.VMEM((tm, d_ff), BF16), pltpu.VMEM((tm, d), F32)]
    if mixer and _MIX_OUT[mixer][2]:
        scratch.append(pltpu.VMEM((tm, d), BF16))
    return pl.pallas_call(
        functools.partial(_ffn_kernel, sub, n_sub, mixer),
        grid=(r // tm,),
        in_specs=[row_spec(x),
                  pl.BlockSpec((None, N_MOD_ROWS, d), lambda i: (row_of(i, tm), 0, 0)),
                  _const_spec(gains.shape),
                  _resident_spec(w_in.shape, in_lead),
                  _resident_spec(w_out.shape, out_lead)]
                 + [row_spec(a) for a in mix_rows] + [_resident_spec(c.shape) for c in mix_consts],
        out_specs=row_spec(x),
        out_shape=jax.ShapeDtypeStruct((r, d), F32),
        scratch_shapes=scratch,
        compiler_params=_params(("parallel",)),
        name=f"ffn{sub}" + (f"_{mixer}" if mixer else ""),
    )(x, mod, gains, w_in, w_out, *mix_rows, *mix_consts)


def _mix0_in_kernel(n_sub, x_ref, mod_ref, g_ref, w_ref, ng_ref, ug_ref, vn_ref, us_ref, h_scr):
    d_a = ug_ref.shape[1]
    rows_per = x_ref.shape[0] // n_sub

    def rows(s):
        return slice(s * rows_per, (s + 1) * rows_per)

    def prologue(s):
        h_scr[rows(s), :] = _modulated(x_ref[rows(s), :], mod_ref, g_ref)

    def project(s):
        r = rows(s)
        h = h_scr[r, :]
        for c0 in range(0, d_a, MXU_COLS):
            ug_ref[r, c0:c0 + MXU_COLS] = jax.nn.gelu(_mm(h, w_ref[:, c0:c0 + MXU_COLS])).astype(BF16)
            vv = jax.nn.gelu(_mm(h, w_ref[:, d_a + c0:d_a + c0 + MXU_COLS]))
            for g0 in range(0, MXU_COLS, SGU_CHUNK):
                v = vv[:, g0:g0 + SGU_CHUNK]
                mu = jnp.mean(v, axis=-1, keepdims=True)
                vc = v - mu
                var = jnp.mean(vc * vc, axis=-1, keepdims=True)
                vn_ref[r, c0 + g0:c0 + g0 + SGU_CHUNK] = (
                    vc * lax.rsqrt(var + NORM_EPS) * ng_ref[:, c0 + g0:c0 + g0 + SGU_CHUNK]).astype(BF16)
        us_ref[r, :] = _mm(h, w_ref[:, 2 * d_a:])

    prologue(0)
    for s in range(n_sub):
        if s + 1 < n_sub:
            prologue(s + 1)
        project(s)


def _mix0_in(x, mod, gains, w_in, norm_g, tm, row_of, name):
    r, d = x.shape
    d_a = norm_g.shape[1]
    d_b = w_in.shape[1] - 2 * d_a
    rows = lambda w: pl.BlockSpec((tm, w), lambda i: (i, 0))
    return pl.pallas_call(
        functools.partial(_mix0_in_kernel, _tiles()["mix_sub"]),
        grid=(r // tm,),
        in_specs=[rows(d),
                  pl.BlockSpec((None, N_MOD_ROWS, d), lambda i: (row_of(i, tm), 0, 0)),
                  _const_spec(gains.shape), _const_spec(w_in.shape), _const_spec(norm_g.shape)],
        out_specs=[rows(d_a), rows(d_a), rows(d_b)],
        out_shape=[jax.ShapeDtypeStruct((r, d_a), BF16)] * 2 + [jax.ShapeDtypeStruct((r, d_b), F32)],
        scratch_shapes=[pltpu.VMEM((tm, d), BF16)],
        compiler_params=_params(("parallel",)),
        name=name,
    )(x, mod, gains, w_in, norm_g)


def _s5_kernel(steps, n_ctx, ufl_ref, ufc_ref, ubl_ref, ubc_ref, bf_ref, bb_ref, cf_ref, cb_ref, lam_ref,
               yfl_ref, yfc_ref, ybl_ref, ybc_ref, sf_scr, sb_scr, hf_scr, hb_scr, r_scr, t_scr):
    nb = V7X_SUBLANES
    n = lam_ref.shape[1]
    lane_slabs = [slice(c, c + V7X_LANES) for c in range(0, ufl_ref.shape[2], V7X_LANES)]
    pitch = r_scr.shape[2] // nb
    in_ctx = pl.program_id(0) < n_ctx

    def batch_rows(b):
        return slice(b * pitch, b * pitch + steps)

    def time_rows(t):
        return slice(t * nb, (t + 1) * nb)

    def to_time_major(use, lat_ref, ctx_ref):
        def load(u_ref):
            for b in range(nb):
                for sl, lanes in enumerate(lane_slabs):
                    r_scr[use, sl, batch_rows(b), :] = u_ref[b, :, lanes]

        pl.when(in_ctx)(lambda: load(ctx_ref))
        pl.when(jnp.logical_not(in_ctx))(lambda: load(lat_ref))
        for t in range(steps):
            for sl, lanes in enumerate(lane_slabs):
                t_scr[use, time_rows(t), lanes] = r_scr[use, sl, pl.ds(t, nb, stride=pitch), :]
        return t_scr[use]

    def from_time_major(use, y, lat_ref, ctx_ref):
        t_scr[use] = y
        for t in range(steps):
            for sl, lanes in enumerate(lane_slabs):
                r_scr[use, sl, pl.ds(t, nb, stride=pitch), :] = t_scr[use, time_rows(t), lanes]

        def store(y_ref):
            for b in range(nb):
                for sl, lanes in enumerate(lane_slabs):
                    y_ref[b, :, lanes] = r_scr[use, sl, batch_rows(b), :]

        pl.when(in_ctx)(lambda: store(ctx_ref))
        pl.when(jnp.logical_not(in_ctx))(lambda: store(lat_ref))

    @pl.when(pl.program_id(0) == 0)
    def _():
        hf_scr[...] = jnp.zeros_like(hf_scr)
        hb_scr[...] = jnp.zeros_like(hb_scr)

    sf_scr[...] = _mm(to_time_major(0, ufl_ref, ufc_ref).astype(BF16), bf_ref[...])
    sb_scr[...] = _mm(to_time_major(1, ubl_ref, ubc_ref).astype(BF16), bb_ref[...])

    def scan(s_scr, h_scr, lam_row, reverse):
        lr = lam_ref[lam_row:lam_row + nb, :]
        li = lam_ref[lam_row + nb:lam_row + 2 * nb, :]
        hr, hi = h_scr[:, 0:n], h_scr[:, n:2 * n]
        for k in range(steps):
            t = (steps - 1 - k) if reverse else k
            rows = slice(t * nb, (t + 1) * nb)
            hr, hi = (lr * hr - li * hi + s_scr[rows, 0:n],
                      lr * hi + li * hr + s_scr[rows, n:2 * n])
            s_scr[rows, 0:n] = hr
            s_scr[rows, n:2 * n] = hi
        h_scr[:, 0:n] = hr
        h_scr[:, n:2 * n] = hi

    scan(sf_scr, hf_scr, 0, False)
    yf = _mm(sf_scr[...].astype(BF16), cf_ref[...])
    scan(sb_scr, hb_scr, 2 * nb, True)
    yb = _mm(sb_scr[...].astype(BF16), cb_ref[...])
    from_time_major(2, yf, yfl_ref, yfc_ref)
    from_time_major(3, yb, ybl_ref, ybc_ref)


def _s5_scan(us_lat, us_ctx, b_mats, c_mats, lam):
    bsz, seq, d_b = us_lat.shape
    nb = V7X_SUBLANES
    steps = _tiles()["s5_steps"]
    n_lat, n_ctx = seq // steps, us_ctx.shape[1] // steps
    n2 = b_mats.shape[-1]
    pitch = steps + nb
    n_slabs = d_b // V7X_LANES
    blk = (bsz, steps, d_b)

    def lat_pos(k):
        return jnp.maximum(k - n_ctx, 0)

    def ctx_pos(k):
        return jnp.minimum(k, n_ctx - 1)

    fwd_lat = pl.BlockSpec(blk, lambda k: (0, lat_pos(k), 0))
    fwd_ctx = pl.BlockSpec(blk, lambda k: (0, ctx_pos(k), 0))
    bwd_lat = pl.BlockSpec(blk, lambda k: (0, n_lat - 1 - lat_pos(k), 0))
    bwd_ctx = pl.BlockSpec(blk, lambda k: (0, n_ctx - 1 - ctx_pos(k), 0))
    yfl, yfc, ybl, ybc = pl.pallas_call(
        functools.partial(_s5_kernel, steps, n_ctx),
        grid=(n_ctx + n_lat,),
        in_specs=[fwd_lat, fwd_ctx, bwd_lat, bwd_ctx,
                  pl.BlockSpec((None, d_b, n2), lambda k: (0, 0, 0)),
                  pl.BlockSpec((None, d_b, n2), lambda k: (1, 0, 0)),
                  pl.BlockSpec((None, n2, d_b), lambda k: (0, 0, 0)),
                  pl.BlockSpec((None, n2, d_b), lambda k: (1, 0, 0)),
                  _const_spec(lam.shape)],
        out_specs=[fwd_lat, fwd_ctx, bwd_lat, bwd_ctx],
        out_shape=[jax.ShapeDtypeStruct(a.shape, F32) for a in (us_lat, us_ctx, us_lat, us_ctx)],
        scratch_shapes=[pltpu.VMEM((steps * nb, n2), F32), pltpu.VMEM((steps * nb, n2), F32),
                        pltpu.VMEM((nb, n2), F32), pltpu.VMEM((nb, n2), F32),
                        pltpu.VMEM((4, n_slabs, nb * pitch, V7X_LANES), F32),
                        pltpu.VMEM((4, steps * nb, d_b), F32)],
        compiler_params=_params(("arbitrary",)),
        name="s5_scan",
    )(us_lat, us_ctx, us_lat, us_ctx, b_mats, b_mats, c_mats, c_mats, lam)
    return (yfl, ybl), (yfc, ybc)


def _s5_operands(lam_re, lam_im, log_step, b_re, b_im, c_re, c_im):
    n_dir, groups, states = lam_re.shape
    gdim = b_re.shape[-1]
    dt = jnp.exp(log_step.astype(F32))[..., None]
    lr, li = lam_re.astype(F32), lam_im.astype(F32)
    mag = jnp.exp(lr * dt)
    ar, ai = mag * jnp.cos(li * dt), mag * jnp.sin(li * dt)
    den = lr * lr + li * li
    fr = ((ar - 1.0) * lr + ai * li) / den
    fi = (ai * lr - (ar - 1.0) * li) / den
    bbr = fr[..., None] * b_re - fi[..., None] * b_im
    bbi = fr[..., None] * b_im + fi[..., None] * b_re
    eye = jnp.eye(groups, dtype=F32)
    n = groups * states

    def in_mat(b):
        return jnp.einsum("dgpc,gh->dgchp", b, eye).reshape(n_dir, groups * gdim, n)

    def out_mat(c):
        return jnp.einsum("dgcp,gh->dgphc", c, eye).reshape(n_dir, n, groups * gdim)

    b_mats = jnp.concatenate([in_mat(bbr), in_mat(bbi)], axis=-1).astype(BF16)
    c_mats = jnp.concatenate([out_mat(c_re.astype(F32)), -out_mat(c_im.astype(F32))], axis=1).astype(BF16)
    nb = V7X_SUBLANES
    lam = jnp.concatenate([jnp.broadcast_to(v.reshape(1, n), (nb, n))
                           for v in (ar[0], ai[0], ar[1], ai[1])], axis=0)
    return b_mats, c_mats, lam


def _rope(x, cos, sin_lo, sin_hi):
    quarter = HEAD_DIM // 4
    return (x * cos + pltpu.roll(x, HEAD_DIM - quarter, axis=1) * sin_lo
            + pltpu.roll(x, quarter, axis=1) * sin_hi)


def _modulated(x, mod_ref, g_ref):
    return (_rms(x) * (g_ref[0:1, :] * (1.0 + mod_ref[4:5, :])) + mod_ref[3:4, :]).astype(BF16)


def _head_slices(h, w_ref, col0, n_heads):
    per_dot = MXU_COLS // HEAD_DIM
    assert n_heads % per_dot == 0
    for pair in range(n_heads // per_dot):
        c0 = col0 + pair * MXU_COLS
        wide = _mm(h, w_ref[:, c0:c0 + MXU_COLS])
        for sub in range(per_dot):
            yield wide[:, sub * HEAD_DIM:(sub + 1) * HEAD_DIM]


def _kv_heads(h, r, w_ref, ng_ref, q_dim, k_ref, vt_ref, rope):
    kvh = vt_ref.shape[0]
    for hd, k in enumerate(_head_slices(h, w_ref, q_dim, kvh)):
        k = _rms(k) * ng_ref[1:2, :]
        if rope is not None:
            k = _rope(k, *rope)
        k_ref[r, hd * HEAD_DIM:(hd + 1) * HEAD_DIM] = k.astype(BF16)
    for hd, v in enumerate(_head_slices(h, w_ref, q_dim + kvh * HEAD_DIM, kvh)):
        vt_ref[hd, 0:HEAD_DIM, r] = v.T.astype(BF16)
        vt_ref[hd, HEAD_DIM:, r] = jnp.ones((V_ONES_ROWS, r.stop - r.start), BF16)


def _kv_ctx_kernel(x_ref, mod_ref, g_ref, w_ref, ng_ref, k_ref, vt_ref):
    q_dim = w_ref.shape[1] - 2 * k_ref.shape[1]
    _kv_heads(_modulated(x_ref[...], mod_ref, g_ref), slice(0, x_ref.shape[0]), w_ref, ng_ref, q_dim,
              k_ref, vt_ref, None)


def _qkv_lat_kernel(n_sub, x_ref, mod_ref, g_ref, w_ref, ng_ref, rope_ref, qtab_ref,
                    qt_ref, k_ref, vt_ref, h_scr):
    heads = qt_ref.shape[0]
    rows_per = x_ref.shape[0] // n_sub
    quarter = HEAD_DIM // 4

    def rows(s):
        return slice(s * rows_per, (s + 1) * rows_per)

    def prologue(s):
        h_scr[rows(s), :] = _modulated(x_ref[rows(s), :], mod_ref, g_ref)

    def project(s):
        r = rows(s)
        h = h_scr[r, :]
        for hd, p in enumerate(_head_slices(h, w_ref, 0, heads)):
            pt = p.T
            inv = lax.rsqrt(jnp.mean(pt * pt, axis=0, keepdims=True) + NORM_EPS)
            partner = jnp.concatenate([pt[quarter:2 * quarter], pt[0:quarter],
                                       pt[3 * quarter:], pt[2 * quarter:3 * quarter]], axis=0)
            qt_ref[hd, :, r] = ((pt * qtab_ref[0, :, r] + partner * qtab_ref[1, :, r]) * inv).astype(BF16)
        rope = tuple(rope_ref[i, r, :] for i in range(3))
        _kv_heads(h, r, w_ref, ng_ref, heads * HEAD_DIM, k_ref, vt_ref, rope)

    prologue(0)
    for s in range(n_sub):
        if s + 1 < n_sub:
            prologue(s + 1)
        project(s)


def _qkv(xl, xc, mod, gains, w_qkv, qk_gains, rope_tab, bsz):
    d = xl.shape[1]
    seq, n_ctx = xl.shape[0] // bsz, xc.shape[0] // bsz
    kv_dim = (w_qkv.shape[1] - d) // 2
    heads, kvh = d // HEAD_DIM, kv_dim // HEAD_DIM
    tm = _tiles()["mix"]
    pos_blocks = seq // tm

    def kv_shapes(n):
        return [jax.ShapeDtypeStruct((bsz, n, kv_dim), BF16), jax.ShapeDtypeStruct((bsz, kvh, V_ROWS, n), BF16)]

    consts = [_const_spec(gains.shape), _const_spec(w_qkv.shape), _const_spec(qk_gains.shape)]
    k_ctx, vt_ctx = pl.pallas_call(
        _kv_ctx_kernel,
        grid=(bsz,),
        in_specs=[pl.BlockSpec((n_ctx, d), lambda b: (b, 0)),
                  pl.BlockSpec((None, N_MOD_ROWS, d), lambda b: (CTX_MOD_ROW, 0, 0))] + consts,
        out_specs=[pl.BlockSpec((None, n_ctx, kv_dim), lambda b: (b, 0, 0)),
                   pl.BlockSpec((None, kvh, V_ROWS, n_ctx), lambda b: (b, 0, 0, 0))],
        out_shape=kv_shapes(n_ctx),
        compiler_params=_params(("parallel",)),
        name="kv_ctx",
    )(xc, mod, gains, w_qkv, qk_gains)
    n_sub = _tiles()["mix_sub"]
    qt, k_lat, vt_lat = pl.pallas_call(
        functools.partial(_qkv_lat_kernel, n_sub),
        grid=(bsz * pos_blocks,),
        in_specs=[pl.BlockSpec((tm, d), lambda i: (i, 0)),
                  pl.BlockSpec((None, N_MOD_ROWS, d), lambda i: (i // pos_blocks, 0, 0))] + consts
                 + [pl.BlockSpec((3, tm, HEAD_DIM), lambda i: (0, i % pos_blocks, 0)),
                    pl.BlockSpec((2, HEAD_DIM, tm), lambda i: (0, 0, i % pos_blocks))],
        out_specs=[pl.BlockSpec((None, heads, HEAD_DIM, tm), lambda i: (i // pos_blocks, 0, 0, i % pos_blocks)),
                   pl.BlockSpec((None, tm, kv_dim), lambda i: (i // pos_blocks, i % pos_blocks, 0)),
                   pl.BlockSpec((None, kvh, V_ROWS, tm), lambda i: (i // pos_blocks, 0, 0, i % pos_blocks))],
        out_shape=[jax.ShapeDtypeStruct((bsz, heads, HEAD_DIM, seq), BF16)] + kv_shapes(seq),
        scratch_shapes=[pltpu.VMEM((tm, d), BF16)],
        compiler_params=_params(("parallel",)),
        name="qkv",
    )(xl, mod, gains, w_qkv, qk_gains, rope_tab, _q_rope_tables(rope_tab, qk_gains[0]))
    return qt, (k_lat, k_ctx), (vt_lat, vt_ctx)


def _q_rope_tables(rope_tab, q_gain):
    quarter = HEAD_DIM // 4
    partner = jnp.concatenate([jnp.arange(quarter, 2 * quarter), jnp.arange(0, quarter),
                               jnp.arange(3 * quarter, 4 * quarter), jnp.arange(2 * quarter, 3 * quarter)])
    scale = HEAD_DIM ** -0.5 * math.log2(math.e)
    cos_g = rope_tab[0] * (q_gain * scale)[None, :]
    sin_g = (rope_tab[1] + rope_tab[2]) * (q_gain[partner] * scale)[None, :]
    return jnp.stack([cos_g.T, sin_g.T])


def _rope_tables(seq):
    rows = seq // GRID_W
    axis_dim = HEAD_DIM // 2
    quarter = axis_dim // 2
    row_id = jnp.repeat(jnp.arange(rows, dtype=F32), GRID_W)
    col_id = jnp.tile(jnp.arange(GRID_W, dtype=F32), rows)
    inv_freq = ROPE_THETA ** (-jnp.arange(0, axis_dim, 2, dtype=F32) / axis_dim)
    a_row, a_col = row_id[:, None] * inv_freq, col_id[:, None] * inv_freq
    zero = jnp.zeros((seq, quarter), F32)
    cos = jnp.concatenate([jnp.cos(a_row)] * 2 + [jnp.cos(a_col)] * 2, axis=1)
    sin_lo = jnp.concatenate([-jnp.sin(a_row), zero, -jnp.sin(a_col), zero], axis=1)
    sin_hi = jnp.concatenate([zero, jnp.sin(a_row), zero, jnp.sin(a_col)], axis=1)
    return jnp.stack([cos, sin_lo, sin_hi])


def _attn_kernel(qt_ref, qn_ref, kl_ref, kc_ref, kln_ref, kcn_ref, vtl_ref, vtc_ref, o_ref, s_scr, m_scr):
    heads = qt_ref.shape[0]
    n_lat = kl_ref.shape[0]
    chunks = [(part, c, min(c + ATTN_KEY_CHUNK, n), base + c)
              for part, n, base in ((0, n_lat, 0), (1, kc_ref.shape[0], n_lat))
              for c in range(0, n, ATTN_KEY_CHUNK)]
    values = (vtl_ref, vtc_ref)

    def scores(q, keys, slot, chunk, m):
        part, c0, c1, r0 = chunk
        s = _mm(keys[part][c0:c1, :], q)
        s_scr[slot, r0:r0 + c1 - c0, :] = s
        cm = jnp.max(s, axis=0, keepdims=True)
        return cm if m is None else jnp.maximum(m, cm)

    @pl.when((pl.program_id(0) == 0) & (pl.program_id(1) == 0) & (pl.program_id(2) == 0))
    def _():
        m0 = None
        for chunk in chunks:
            m0 = scores(qt_ref[0], (kl_ref, kc_ref), 0, chunk, m0)
        m_scr[...] = m0

    m = m_scr[...]
    for h in range(heads):
        last = h + 1 == heads
        q_next, keys_next = (qn_ref[...], (kln_ref, kcn_ref)) if last else (qt_ref[h + 1], (kl_ref, kc_ref))
        m_next, acc = None, None
        for chunk in chunks:
            part, c0, c1, r0 = chunk
            m_next = scores(q_next, keys_next, (h + 1) % 2, chunk, m_next)
            p = jnp.exp2(s_scr[h % 2, r0:r0 + c1 - c0, :] - m).astype(BF16)
            pv = _mm(values[part][:, c0:c1], p)
            acc = pv if acc is None else acc + pv
        o = acc[0:HEAD_DIM, :] / acc[HEAD_DIM:HEAD_DIM + 1, :]
        o_ref[:, h * HEAD_DIM:(h + 1) * HEAD_DIM] = o.T.astype(BF16)
        m = m_next
    m_scr[...] = m


def _attention(qt, k_parts, vt_parts):
    bsz, heads, _, t = qt.shape
    kvh = vt_parts[0].shape[1]
    l = sum(k.shape[1] for k in k_parts)
    q_per_kv = heads // kvh
    assert q_per_kv % 2 == 0, "score buffers alternate per head and must line up across grid steps"
    tq = _tiles()["attn_q"]
    nq = t // tq
    steps = bsz * kvh * nq

    def following(b, j, i):
        lin = jnp.minimum((b * kvh + j) * nq + i + 1, steps - 1)
        return lin // (kvh * nq), (lin // nq) % kvh, lin % nq

    def q_next_map(b, j, i):
        b2, j2, i2 = following(b, j, i)
        return b2, j2 * q_per_kv, 0, i2

    def k_next_map(b, j, i):
        b2, j2, _ = following(b, j, i)
        return b2, 0, j2

    return pl.pallas_call(
        _attn_kernel,
        grid=(bsz, kvh, nq),
        in_specs=[pl.BlockSpec((None, q_per_kv, HEAD_DIM, tq), lambda b, j, i: (b, j, 0, i)),
                  pl.BlockSpec((None, None, HEAD_DIM, tq), q_next_map)]
                 + [pl.BlockSpec((None, k.shape[1], HEAD_DIM), lambda b, j, i: (b, 0, j)) for k in k_parts]
                 + [pl.BlockSpec((None, k.shape[1], HEAD_DIM), k_next_map) for k in k_parts]
                 + [pl.BlockSpec((None, None, V_ROWS, v.shape[3]), lambda b, j, i: (b, j, 0, 0)) for v in vt_parts],
        out_specs=pl.BlockSpec((None, tq, q_per_kv * HEAD_DIM), lambda b, j, i: (b, i, j)),
        out_shape=jax.ShapeDtypeStruct((bsz, t, heads * HEAD_DIM), BF16),
        scratch_shapes=[pltpu.VMEM((2, l, tq), F32), pltpu.VMEM((1, tq), F32)],
        compiler_params=_params(("arbitrary",) * 3),
        name="attention",
    )(qt, qt, *k_parts, *k_parts, *vt_parts)


def kernel(x, c, ctx, c_ctx, w_mod, b_mod, norm_pre, norm_post, ffn_w_in, ffn_w_out, ab_w_in, ab_w_out, sgu_norm_g, sgu_w, sgu_b, s5_lam_re, s5_lam_im, s5_log_step, s5_b_re, s5_b_im, s5_c_re, s5_c_im, s5_d, s5_glu_w, s5_glu_b, attn_w_qkv, attn_w_out, attn_q_norm, attn_k_norm):
    bsz, seq, d = x.shape
    n_ctx = ctx.shape[1]
    depth = w_mod.shape[0]
    assert bsz == V7X_SUBLANES, "the S5 scan keeps the batch on the sublane axis"
    d_a = sgu_norm_g.shape[1]
    d_b = s5_d.shape[1]
    kv_dim = (attn_w_qkv.shape[2] - d) // 2
    q_per_kv = d // kv_dim

    def lat_row(i, tm):
        return (i * tm) // seq

    def ctx_row(i, tm):
        return CTX_MOD_ROW

    cond = jnp.zeros((N_MOD_ROWS, d), F32).at[:bsz].set(c).at[CTX_MOD_ROW].set(c_ctx)
    mod_all = _modulation(cond, w_mod, b_mod).reshape(depth, N_MOD_ROWS, -1, d)
    mod_all = jnp.pad(mod_all, ((0, 0), (0, 0), (0, N_MOD_ROWS - mod_all.shape[2]), (0, 0)))

    xl = x.reshape(bsz * seq, d)
    xc = ctx.reshape(bsz * n_ctx, d)
    w_in_all, w_out_all = ffn_w_in.astype(BF16), ffn_w_out.astype(BF16)

    for i in range(depth):
        last = i == depth - 1
        j = i // 2
        mod = mod_all[i]
        gains = [jnp.zeros((V7X_SUBLANES, d), F32).at[0].set(norm_pre[i, s]).at[1].set(norm_post[i, s])
                 for s in range(3)]
        w_in1, w_in2 = (w_in_all, (i, 0)), (w_in_all, (i, 1))
        w_out1, w_out2 = (w_out_all, (i, 0)), (w_out_all, (i, 1))

        xl = _ffn(xl, mod, gains[0], w_in1, w_out1, 0, lat_row)
        xc = _ffn(xc, mod, gains[0], w_in1, w_out1, 0, ctx_row)

        if i % 2 == 0:
            w_in = ab_w_in[j].astype(BF16)
            norm_g = sgu_norm_g[j].reshape(1, d_a)
            ug_l, vn_l, us_l = _mix0_in(xl, mod, gains[1], w_in, norm_g, _tiles()["mix"], lat_row, "mix0_in")
            ug_c, vn_c, us_c = _mix0_in(xc, mod, gains[1], w_in, norm_g, n_ctx, ctx_row, "mix0_in_ctx")
            b_mats, c_mats, lam = _s5_operands(s5_lam_re[j], s5_lam_im[j], s5_log_step[j], s5_b_re[j],
                                               s5_b_im[j], s5_c_re[j], s5_c_im[j])
            y_l, y_c = _s5_scan(us_l.reshape(bsz, seq, d_b), us_c.reshape(bsz, n_ctx, d_b), b_mats, c_mats, lam)
            consts = (sgu_w[j].astype(BF16),
                      jnp.broadcast_to(sgu_b[j][:, :, None], sgu_w[j].shape).astype(F32),
                      s5_d[j].reshape(1, d_b), s5_glu_w[j].astype(BF16), s5_glu_b[j].reshape(1, d_b),
                      ab_w_out[j].astype(BF16))
            mixer = "sgu_s5"
            mix_l = (ug_l, vn_l) + tuple(y.reshape(bsz * seq, d_b) for y in y_l) + (us_l,)
            mix_c = (ug_c, vn_c) + tuple(y.reshape(bsz * n_ctx, d_b) for y in y_c) + (us_c,)
            ctx_tile = (n_ctx, 2)
        else:
            if not last:
                raise NotImplementedError("context stream through an attention layer")
            w_qkv = attn_w_qkv[j].astype(BF16)
            qk_gains = jnp.zeros((V7X_SUBLANES, HEAD_DIM), F32).at[0].set(attn_q_norm[j]).at[1].set(attn_k_norm[j])
            rope_tab = _rope_tables(seq)
            qt, k, vt = _qkv(xl, xc, mod, gains[1], w_qkv, qk_gains, rope_tab, bsz)
            mixer = "attn"
            mix_l, mix_c = (_attention(qt, k, vt).reshape(bsz * seq, d),), None
            consts = (attn_w_out[j].astype(BF16),)
            ctx_tile = None

        gains2 = gains[2].at[2].set(norm_post[i, 1])
        xl = _ffn(xl, mod, gains2, w_in2, w_out2, 2, lat_row, mixer, mix_l, consts)
        if not last:
            xc = _ffn(xc, mod, gains2, w_in2, w_out2, 2, ctx_row, mixer, mix_c, consts, ctx_tile)

    return xl.reshape(bsz, seq, d)
```

```python
import functools
import math

import jax
import jax.numpy as jnp
from jax import lax
from jax.experimental import pallas as pl
from jax.experimental.pallas import tpu as pltpu

F32 = jnp.float32
BF16 = jnp.bfloat16

NORM_EPS = 1e-6
FFN_RES = 0.5
GRID_W = 64
ROPE_THETA = 10000.0
SGU_CHUNK = 128
HEAD_DIM = 128
ATTN_KEY_CHUNK = 512
V_ONES_ROWS = 16
V_ROWS = HEAD_DIM + V_ONES_ROWS

V7X_LANES = 128
V7X_SUBLANES = 8
MXU_COLS = 256
V7X_VMEM_BYTES = 64 * 1024 * 1024
VMEM_LIMIT = 56 * 1024 * 1024

N_MOD_ROWS = 16
CTX_MOD_ROW = 8


def _tiles():
    return dict(ffn=(1024, 4), ffn_attn=(1024, 4), ffn_sgu_s5=(512, 2), mix=1024, mix_sub=4, attn_q=512, s5_steps=64, mod_n=2304)


def _params(sem, vmem=VMEM_LIMIT):
    return pltpu.CompilerParams(dimension_semantics=sem, vmem_limit_bytes=vmem)


def _const_spec(shape):
    nd = len(shape)
    return pl.BlockSpec(shape, lambda *_: (0,) * nd)


def _resident_spec(shape, lead=()):
    nd = len(shape) - len(lead)
    return pl.BlockSpec((None,) * len(lead) + tuple(shape[len(lead):]),
                        lambda *_: tuple(lead) + (0,) * nd, pipeline_mode=pl.Buffered(1))


def _rms(x):
    return x * lax.rsqrt(jnp.mean(x * x, axis=-1, keepdims=True) + NORM_EPS)


def _mm(a, b):
    return jnp.dot(a, b, preferred_element_type=F32)


def _mod_kernel(c_ref, w_ref, b_ref, o_ref):
    c = c_ref[...]
    s = (c * jax.nn.sigmoid(c)).astype(BF16)
    o_ref[...] = _mm(s, w_ref[...].astype(BF16)) + b_ref[...]


def _modulation(cond, w_mod, b_mod):
    depth, d, n = w_mod.shape
    tn = _tiles()["mod_n"]
    return pl.pallas_call(
        _mod_kernel,
        grid=(depth, n // tn),
        in_specs=[_const_spec((N_MOD_ROWS, d)),
                  pl.BlockSpec((None, d, tn), lambda l, j: (l, 0, j)),
                  pl.BlockSpec((None, 1, tn), lambda l, j: (l, 0, j))],
        out_specs=pl.BlockSpec((None, N_MOD_ROWS, tn), lambda l, j: (l, 0, j)),
        out_shape=jax.ShapeDtypeStruct((depth, N_MOD_ROWS, n), F32),
        compiler_params=_params(("parallel", "parallel")),
        name="modulation",
    )(cond, w_mod, b_mod.reshape(depth, 1, n))


def _ffn_chunks(d_ff, width=512):
    return [(c, min(c + width, d_ff)) for c in range(0, d_ff, width)]


def _attn_mix_out(r, x_ref, mod_ref, g_ref, mix_refs, z_scr):
    a_ref, wo_ref = mix_refs
    y = _mm(a_ref[r, :], wo_ref[...])
    return x_ref[r, :] + mod_ref[5:6, :] * (_rms(y) * g_ref[2:3, :])


def _sgu_s5_mix_out(r, x_ref, mod_ref, g_ref, mix_refs, z_scr):
    ug_ref, vn_ref, yf_ref, yb_ref, us_ref, sw_ref, sb_ref, d_ref, gw_ref, gb_ref, wo_ref = mix_refs
    d_a = ug_ref.shape[1]
    row_chunks = range(r.start, r.stop, SGU_CHUNK)
    for g in range(d_a // SGU_CHUNK):
        c0 = g * SGU_CHUNK
        vn_wide = jnp.concatenate([vn_ref[r0:r0 + SGU_CHUNK, c0:c0 + SGU_CHUNK] for r0 in row_chunks], axis=1)
        mixed_wide = _mm(sw_ref[g], vn_wide)
        for n, r0 in enumerate(row_chunks):
            mixed = mixed_wide[:, n * SGU_CHUNK:(n + 1) * SGU_CHUNK] + sb_ref[g]
            z_scr[n * SGU_CHUNK:(n + 1) * SGU_CHUNK, c0:c0 + SGU_CHUNK] = (
                ug_ref[r0:r0 + SGU_CHUNK, c0:c0 + SGU_CHUNK].astype(F32) * mixed).astype(BF16)
    ys = jax.nn.gelu(yf_ref[r, :] + yb_ref[r, :] + d_ref[...] * us_ref[r, :])
    gl = jax.nn.sigmoid(_mm(ys.astype(BF16), gw_ref[...]) + gb_ref[...])
    z_scr[:, d_a:] = (ys * gl).astype(BF16)
    y = _mm(z_scr[...], wo_ref[...])
    return x_ref[r, :] + mod_ref[5:6, :] * (_rms(y) * g_ref[2:3, :])


_MIX_OUT = {"attn": (_attn_mix_out, 2, False), "sgu_s5": (_sgu_s5_mix_out, 11, True)}


def _ffn_kernel(sub, n_sub, mixer, x_ref, mod_ref, g_ref, win_ref, wout_ref, *rest):
    mix_fn, n_mix, needs_z = _MIX_OUT[mixer] if mixer else (None, 0, False)
    mix_refs, rest = rest[:n_mix], rest[n_mix:]
    o_ref, h_scr, a_scr, y_scr = rest[:4]
    z_scr = rest[4] if needs_z else None
    d_ff = wout_ref.shape[0]
    rows_per = x_ref.shape[0] // n_sub
    shift, scale, gate = (mod_ref[3 * sub + k:3 * sub + k + 1, :] for k in range(3))
    pre_gain = g_ref[0:1, :] * (1.0 + scale)
    post_gain = (FFN_RES * gate) * g_ref[1:2, :]

    def rows(s):
        return slice(s * rows_per, (s + 1) * rows_per)

    def prologue(s):
        if mix_fn is None:
            x = x_ref[rows(s), :]
        else:
            x = mix_fn(rows(s), x_ref, mod_ref, g_ref, mix_refs, z_scr.at[s % 2] if needs_z else None)
            o_ref[rows(s), :] = x
        h_scr[s % 2] = (_rms(x) * pre_gain + shift).astype(BF16)

    def up_chunk(s, c0, c1):
        h = h_scr[s % 2]
        g = _mm(h, win_ref[:, c0:c1])
        u = _mm(h, win_ref[:, d_ff + c0:d_ff + c1])
        a_scr[s % 2, :, c0:c1] = (g * jax.nn.sigmoid(g) * u).astype(BF16)

    def epilogue(s):
        base = x_ref if mix_fn is None else o_ref
        o_ref[rows(s), :] = base[rows(s), :] + post_gain * _rms(y_scr[s % 2])

    prologue(0)
    for s in range(n_sub):
        for ci, (c0, c1) in enumerate(_ffn_chunks(d_ff)):
            up_chunk(s, c0, c1)
            if ci == 0 and s > 0:
                epilogue(s - 1)
            if ci == 2 and s + 1 < n_sub:
                prologue(s + 1)
        y_scr[s % 2] = _mm(a_scr[s % 2], wout_ref[...])
    epilogue(n_sub - 1)


def _ffn(x, mod, gains, w_in, w_out, sub, row_of, mixer=None, mix_rows=(), mix_consts=(), tile=None):
    r, d = x.shape
    (w_in, in_lead), (w_out, out_lead) = w_in, w_out
    d_ff = w_out.shape[-2]
    tm, n_sub = tile or _tiles()[f"ffn_{mixer}" if mixer else "ffn"]

    row_spec = lambda a: pl.BlockSpec((tm, a.shape[1]), lambda i: (i, 0))
    sub_rows = tm // n_sub
    scratch = [pltpu.VMEM((2, sub_rows, d), BF16), pltpu.VMEM((2, sub_rows, d_ff), BF16),
               pltpu.VMEM((2, sub_rows, d), F32)]
    if mixer and _MIX_OUT[mixer][2]:
        scratch.append(pltpu.VMEM((2, sub_rows, d), BF16))
    return pl.pallas_call(
        functools.partial(_ffn_kernel, sub, n_sub, mixer),
        grid=(r // tm,),
        in_specs=[row_spec(x),
                  pl.BlockSpec((None, N_MOD_ROWS, d), lambda i: (row_of(i, tm), 0, 0)),
                  _const_spec(gains.shape),
                  _resident_spec(w_in.shape, in_lead),
                  _resident_spec(w_out.shape, out_lead)]
                 + [row_spec(a) for a in mix_rows] + [_resident_spec(c.shape) for c in mix_consts],
        out_specs=row_spec(x),
        out_shape=jax.ShapeDtypeStruct((r, d), F32),
        scratch_shapes=scratch,
        compiler_params=_params(("parallel",)),
        name=f"ffn{sub}" + (f"_{mixer}" if mixer else ""),
    )(x, mod, gains, w_in, w_out, *mix_rows, *mix_consts)


def _mix0_in_kernel(n_sub, x_ref, mod_ref, g_ref, w_ref, ng_ref, ug_ref, vn_ref, us_ref, h_scr):
    d_a = ug_ref.shape[1]
    rows_per = x_ref.shape[0] // n_sub

    def rows(s):
        return slice(s * rows_per, (s + 1) * rows_per)

    def prologue(s):
        h_scr[rows(s), :] = _modulated(x_ref[rows(s), :], mod_ref, g_ref)

    def project(s):
        r = rows(s)
        h = h_scr[r, :]
        for c0 in range(0, d_a, MXU_COLS):
            ug_ref[r, c0:c0 + MXU_COLS] = jax.nn.gelu(_mm(h, w_ref[:, c0:c0 + MXU_COLS])).astype(BF16)
            vv = jax.nn.gelu(_mm(h, w_ref[:, d_a + c0:d_a + c0 + MXU_COLS]))
            for g0 in range(0, MXU_COLS, SGU_CHUNK):
                v = vv[:, g0:g0 + SGU_CHUNK]
                mu = jnp.mean(v, axis=-1, keepdims=True)
                vc = v - mu
                var = jnp.mean(vc * vc, axis=-1, keepdims=True)
                vn_ref[r, c0 + g0:c0 + g0 + SGU_CHUNK] = (
                    vc * lax.rsqrt(var + NORM_EPS) * ng_ref[:, c0 + g0:c0 + g0 + SGU_CHUNK]).astype(BF16)
        us_ref[r, :] = _mm(h, w_ref[:, 2 * d_a:])

    prologue(0)
    for s in range(n_sub):
        if s + 1 < n_sub:
            prologue(s + 1)
        project(s)


def _mix0_in(x, mod, gains, w_in, norm_g, tm, row_of, name):
    r, d = x.shape
    d_a = norm_g.shape[1]
    d_b = w_in.shape[1] - 2 * d_a
    rows = lambda w: pl.BlockSpec((tm, w), lambda i: (i, 0))
    return pl.pallas_call(
        functools.partial(_mix0_in_kernel, _tiles()["mix_sub"]),
        grid=(r // tm,),
        in_specs=[rows(d),
                  pl.BlockSpec((None, N_MOD_ROWS, d), lambda i: (row_of(i, tm), 0, 0)),
                  _const_spec(gains.shape), _const_spec(w_in.shape), _const_spec(norm_g.shape)],
        out_specs=[rows(d_a), rows(d_a), rows(d_b)],
        out_shape=[jax.ShapeDtypeStruct((r, d_a), BF16)] * 2 + [jax.ShapeDtypeStruct((r, d_b), F32)],
        scratch_shapes=[pltpu.VMEM((tm, d), BF16)],
        compiler_params=_params(("parallel",)),
        name=name,
    )(x, mod, gains, w_in, norm_g)


def _s5_kernel(steps, uf_ref, ub_ref, h0_ref, bf_ref, bb_ref, cf_ref, cb_ref, lam_ref,
               yf_ref, yb_ref, h_ref, sf_scr, sb_scr, hf_scr, hb_scr, r_scr, t_scr):
    nb = V7X_SUBLANES
    n = lam_ref.shape[1]
    lane_slabs = [slice(c, c + V7X_LANES) for c in range(0, uf_ref.shape[2], V7X_LANES)]
    pitch = r_scr.shape[2] // nb

    def batch_rows(b):
        return slice(b * pitch, b * pitch + steps)

    def time_rows(t):
        return slice(t * nb, (t + 1) * nb)

    def to_time_major(use, u_ref):
        for b in range(nb):
            for sl, lanes in enumerate(lane_slabs):
                r_scr[use, sl, batch_rows(b), :] = u_ref[b, :, lanes]
        for t in range(steps):
            for sl, lanes in enumerate(lane_slabs):
                t_scr[use, time_rows(t), lanes] = r_scr[use, sl, pl.ds(t, nb, stride=pitch), :]
        return t_scr[use]

    def from_time_major(use, y, y_ref):
        t_scr[use] = y
        for t in range(steps):
            for sl, lanes in enumerate(lane_slabs):
                r_scr[use, sl, pl.ds(t, nb, stride=pitch), :] = t_scr[use, time_rows(t), lanes]
        for b in range(nb):
            for sl, lanes in enumerate(lane_slabs):
                y_ref[b, :, lanes] = r_scr[use, sl, batch_rows(b), :]

    @pl.when(pl.program_id(0) == 0)
    def _():
        hf_scr[...] = h0_ref[0]
        hb_scr[...] = h0_ref[1]

    sf_scr[...] = _mm(to_time_major(0, uf_ref).astype(BF16), bf_ref[...])
    sb_scr[...] = _mm(to_time_major(1, ub_ref).astype(BF16), bb_ref[...])

    def scan(s_scr, h_scr, lam_row, reverse):
        lr = lam_ref[lam_row:lam_row + nb, :]
        li = lam_ref[lam_row + nb:lam_row + 2 * nb, :]
        hr, hi = h_scr[:, 0:n], h_scr[:, n:2 * n]
        for k in range(steps):
            t = (steps - 1 - k) if reverse else k
            rows = slice(t * nb, (t + 1) * nb)
            hr, hi = (lr * hr - li * hi + s_scr[rows, 0:n],
                      lr * hi + li * hr + s_scr[rows, n:2 * n])
            s_scr[rows, 0:n] = hr
            s_scr[rows, n:2 * n] = hi
        h_scr[:, 0:n] = hr
        h_scr[:, n:2 * n] = hi

    scan(sf_scr, hf_scr, 0, False)
    yf = _mm(sf_scr[...].astype(BF16), cf_ref[...])
    scan(sb_scr, hb_scr, 2 * nb, True)
    yb = _mm(sb_scr[...].astype(BF16), cb_ref[...])
    from_time_major(2, yf, yf_ref)
    from_time_major(3, yb, yb_ref)
    h_ref[0] = hf_scr[...]
    h_ref[1] = hb_scr[...]


def _s5_scan(us, h0, b_mats, c_mats, lam, name):
    bsz, t, d_b = us.shape
    nb = V7X_SUBLANES
    steps = _tiles()["s5_steps"]
    n_chunks = t // steps
    n2 = b_mats.shape[-1]
    pitch = steps + nb
    n_slabs = d_b // V7X_LANES
    fwd = pl.BlockSpec((bsz, steps, d_b), lambda k: (0, k, 0))
    bwd = pl.BlockSpec((bsz, steps, d_b), lambda k: (0, n_chunks - 1 - k, 0))
    return pl.pallas_call(
        functools.partial(_s5_kernel, steps),
        grid=(n_chunks,),
        in_specs=[fwd, bwd, _const_spec(h0.shape),
                  pl.BlockSpec((None, d_b, n2), lambda k: (0, 0, 0)),
                  pl.BlockSpec((None, d_b, n2), lambda k: (1, 0, 0)),
                  pl.BlockSpec((None, n2, d_b), lambda k: (0, 0, 0)),
                  pl.BlockSpec((None, n2, d_b), lambda k: (1, 0, 0)),
                  _const_spec(lam.shape)],
        out_specs=[fwd, bwd, _const_spec(h0.shape)],
        out_shape=[jax.ShapeDtypeStruct(us.shape, F32)] * 2 + [jax.ShapeDtypeStruct(h0.shape, F32)],
        scratch_shapes=[pltpu.VMEM((steps * nb, n2), F32), pltpu.VMEM((steps * nb, n2), F32),
                        pltpu.VMEM((nb, n2), F32), pltpu.VMEM((nb, n2), F32),
                        pltpu.VMEM((4, n_slabs, nb * pitch, V7X_LANES), F32),
                        pltpu.VMEM((4, steps * nb, d_b), F32)],
        compiler_params=_params(("arbitrary",)),
        name=name,
    )(us, us, h0, b_mats, b_mats, c_mats, c_mats, lam)


def _s5_operands(lam_re, lam_im, log_step, b_re, b_im, c_re, c_im):
    n_dir, groups, states = lam_re.shape
    gdim = b_re.shape[-1]
    dt = jnp.exp(log_step.astype(F32))[..., None]
    lr, li = lam_re.astype(F32), lam_im.astype(F32)
    mag = jnp.exp(lr * dt)
    ar, ai = mag * jnp.cos(li * dt), mag * jnp.sin(li * dt)
    den = lr * lr + li * li
    fr = ((ar - 1.0) * lr + ai * li) / den
    fi = (ai * lr - (ar - 1.0) * li) / den
    bbr = fr[..., None] * b_re - fi[..., None] * b_im
    bbi = fr[..., None] * b_im + fi[..., None] * b_re
    eye = jnp.eye(groups, dtype=F32)
    n = groups * states

    def in_mat(b):
        return jnp.einsum("dgpc,gh->dgchp", b, eye).reshape(n_dir, groups * gdim, n)

    def out_mat(c):
        return jnp.einsum("dgcp,gh->dgphc", c, eye).reshape(n_dir, n, groups * gdim)

    b_mats = jnp.concatenate([in_mat(bbr), in_mat(bbi)], axis=-1).astype(BF16)
    c_mats = jnp.concatenate([out_mat(c_re.astype(F32)), -out_mat(c_im.astype(F32))], axis=1).astype(BF16)
    nb = V7X_SUBLANES
    lam = jnp.concatenate([jnp.broadcast_to(v.reshape(1, n), (nb, n))
                           for v in (ar[0], ai[0], ar[1], ai[1])], axis=0)
    return b_mats, c_mats, lam


def _rope(x, cos, sin_lo, sin_hi):
    quarter = HEAD_DIM // 4
    return (x * cos + pltpu.roll(x, HEAD_DIM - quarter, axis=1) * sin_lo
            + pltpu.roll(x, quarter, axis=1) * sin_hi)


def _modulated(x, mod_ref, g_ref):
    return (_rms(x) * (g_ref[0:1, :] * (1.0 + mod_ref[4:5, :])) + mod_ref[3:4, :]).astype(BF16)


def _head_slices(h, w_ref, col0, n_heads):
    per_dot = MXU_COLS // HEAD_DIM
    assert n_heads % per_dot == 0
    for pair in range(n_heads // per_dot):
        c0 = col0 + pair * MXU_COLS
        wide = _mm(h, w_ref[:, c0:c0 + MXU_COLS])
        for sub in range(per_dot):
            yield wide[:, sub * HEAD_DIM:(sub + 1) * HEAD_DIM]


def _kv_heads(h, r, w_ref, ng_ref, q_dim, k_ref, vt_ref, rope):
    kvh = vt_ref.shape[0]
    for hd, k in enumerate(_head_slices(h, w_ref, q_dim, kvh)):
        k = _rms(k) * ng_ref[1:2, :]
        if rope is not None:
            k = _rope(k, *rope)
        k_ref[r, hd * HEAD_DIM:(hd + 1) * HEAD_DIM] = k.astype(BF16)
    for hd, v in enumerate(_head_slices(h, w_ref, q_dim + kvh * HEAD_DIM, kvh)):
        vt_ref[hd, 0:HEAD_DIM, r] = v.T.astype(BF16)
        vt_ref[hd, HEAD_DIM:, r] = jnp.ones((V_ONES_ROWS, r.stop - r.start), BF16)


def _kv_ctx_kernel(x_ref, mod_ref, g_ref, w_ref, ng_ref, k_ref, vt_ref):
    q_dim = w_ref.shape[1] - 2 * k_ref.shape[1]
    _kv_heads(_modulated(x_ref[...], mod_ref, g_ref), slice(0, x_ref.shape[0]), w_ref, ng_ref, q_dim,
              k_ref, vt_ref, None)


def _qkv_lat_kernel(n_sub, x_ref, mod_ref, g_ref, w_ref, ng_ref, rope_ref, qtab_ref,
                    qt_ref, k_ref, vt_ref, h_scr):
    heads = qt_ref.shape[0]
    rows_per = x_ref.shape[0] // n_sub
    quarter = HEAD_DIM // 4

    def rows(s):
        return slice(s * rows_per, (s + 1) * rows_per)

    def prologue(s):
        h_scr[rows(s), :] = _modulated(x_ref[rows(s), :], mod_ref, g_ref)

    def project(s):
        r = rows(s)
        h = h_scr[r, :]
        for hd, p in enumerate(_head_slices(h, w_ref, 0, heads)):
            pt = p.T
            inv = lax.rsqrt(jnp.mean(pt * pt, axis=0, keepdims=True) + NORM_EPS)
            partner = jnp.concatenate([pt[quarter:2 * quarter], pt[0:quarter],
                                       pt[3 * quarter:], pt[2 * quarter:3 * quarter]], axis=0)
            qt_ref[hd, :, r] = ((pt * qtab_ref[0, :, r] + partner * qtab_ref[1, :, r]) * inv).astype(BF16)
        rope = tuple(rope_ref[i, r, :] for i in range(3))
        _kv_heads(h, r, w_ref, ng_ref, heads * HEAD_DIM, k_ref, vt_ref, rope)

    prologue(0)
    for s in range(n_sub):
        if s + 1 < n_sub:
            prologue(s + 1)
        project(s)


def _qkv(xl, xc, mod, gains, w_qkv, qk_gains, rope_tab, bsz):
    d = xl.shape[1]
    seq, n_ctx = xl.shape[0] // bsz, xc.shape[0] // bsz
    kv_dim = (w_qkv.shape[1] - d) // 2
    heads, kvh = d // HEAD_DIM, kv_dim // HEAD_DIM
    tm = _tiles()["mix"]
    pos_blocks = seq // tm

    def kv_shapes(n):
        return [jax.ShapeDtypeStruct((bsz, n, kv_dim), BF16), jax.ShapeDtypeStruct((bsz, kvh, V_ROWS, n), BF16)]

    consts = [_const_spec(gains.shape), _const_spec(w_qkv.shape), _const_spec(qk_gains.shape)]
    k_ctx, vt_ctx = pl.pallas_call(
        _kv_ctx_kernel,
        grid=(bsz,),
        in_specs=[pl.BlockSpec((n_ctx, d), lambda b: (b, 0)),
                  pl.BlockSpec((None, N_MOD_ROWS, d), lambda b: (CTX_MOD_ROW, 0, 0))] + consts,
        out_specs=[pl.BlockSpec((None, n_ctx, kv_dim), lambda b: (b, 0, 0)),
                   pl.BlockSpec((None, kvh, V_ROWS, n_ctx), lambda b: (b, 0, 0, 0))],
        out_shape=kv_shapes(n_ctx),
        compiler_params=_params(("parallel",)),
        name="kv_ctx",
    )(xc, mod, gains, w_qkv, qk_gains)
    n_sub = _tiles()["mix_sub"]
    qt, k_lat, vt_lat = pl.pallas_call(
        functools.partial(_qkv_lat_kernel, n_sub),
        grid=(bsz * pos_blocks,),
        in_specs=[pl.BlockSpec((tm, d), lambda i: (i, 0)),
                  pl.BlockSpec((None, N_MOD_ROWS, d), lambda i: (i // pos_blocks, 0, 0))] + consts
                 + [pl.BlockSpec((3, tm, HEAD_DIM), lambda i: (0, i % pos_blocks, 0)),
                    pl.BlockSpec((2, HEAD_DIM, tm), lambda i: (0, 0, i % pos_blocks))],
        out_specs=[pl.BlockSpec((None, heads, HEAD_DIM, tm), lambda i: (i // pos_blocks, 0, 0, i % pos_blocks)),
                   pl.BlockSpec((None, tm, kv_dim), lambda i: (i // pos_blocks, i % pos_blocks, 0)),
                   pl.BlockSpec((None, kvh, V_ROWS, tm), lambda i: (i // pos_blocks, 0, 0, i % pos_blocks))],
        out_shape=[jax.ShapeDtypeStruct((bsz, heads, HEAD_DIM, seq), BF16)] + kv_shapes(seq),
        scratch_shapes=[pltpu.VMEM((tm, d), BF16)],
        compiler_params=_params(("parallel",)),
        name="qkv",
    )(xl, mod, gains, w_qkv, qk_gains, rope_tab, _q_rope_tables(rope_tab, qk_gains[0]))
    return qt, (k_lat, k_ctx), (vt_lat, vt_ctx)


def _q_rope_tables(rope_tab, q_gain):
    quarter = HEAD_DIM // 4
    partner = jnp.concatenate([jnp.arange(quarter, 2 * quarter), jnp.arange(0, quarter),
                               jnp.arange(3 * quarter, 4 * quarter), jnp.arange(2 * quarter, 3 * quarter)])
    scale = HEAD_DIM ** -0.5 * math.log2(math.e)
    cos_g = rope_tab[0] * (q_gain * scale)[None, :]
    sin_g = (rope_tab[1] + rope_tab[2]) * (q_gain[partner] * scale)[None, :]
    return jnp.stack([cos_g.T, sin_g.T])


def _rope_tables(seq):
    rows = seq // GRID_W
    axis_dim = HEAD_DIM // 2
    quarter = axis_dim // 2
    row_id = jnp.repeat(jnp.arange(rows, dtype=F32), GRID_W)
    col_id = jnp.tile(jnp.arange(GRID_W, dtype=F32), rows)
    inv_freq = ROPE_THETA ** (-jnp.arange(0, axis_dim, 2, dtype=F32) / axis_dim)
    a_row, a_col = row_id[:, None] * inv_freq, col_id[:, None] * inv_freq
    zero = jnp.zeros((seq, quarter), F32)
    cos = jnp.concatenate([jnp.cos(a_row)] * 2 + [jnp.cos(a_col)] * 2, axis=1)
    sin_lo = jnp.concatenate([-jnp.sin(a_row), zero, -jnp.sin(a_col), zero], axis=1)
    sin_hi = jnp.concatenate([zero, jnp.sin(a_row), zero, jnp.sin(a_col)], axis=1)
    return jnp.stack([cos, sin_lo, sin_hi])


def _attn_kernel(qt_ref, qn_ref, kl_ref, kc_ref, kln_ref, kcn_ref, vtl_ref, vtc_ref, o_ref, s_scr, m_scr):
    heads = qt_ref.shape[0]
    n_lat = kl_ref.shape[0]
    chunks = [(part, c, min(c + ATTN_KEY_CHUNK, n), base + c)
              for part, n, base in ((0, n_lat, 0), (1, kc_ref.shape[0], n_lat))
              for c in range(0, n, ATTN_KEY_CHUNK)]
    values = (vtl_ref, vtc_ref)

    def scores(q, keys, slot, chunk, m):
        part, c0, c1, r0 = chunk
        s = _mm(keys[part][c0:c1, :], q)
        s_scr[slot, r0:r0 + c1 - c0, :] = s
        cm = jnp.max(s, axis=0, keepdims=True)
        return cm if m is None else jnp.maximum(m, cm)

    @pl.when((pl.program_id(0) == 0) & (pl.program_id(1) == 0) & (pl.program_id(2) == 0))
    def _():
        m0 = None
        for chunk in chunks:
            m0 = scores(qt_ref[0], (kl_ref, kc_ref), 0, chunk, m0)
        m_scr[...] = m0

    m = m_scr[...]
    for h in range(heads):
        last = h + 1 == heads
        q_next, keys_next = (qn_ref[...], (kln_ref, kcn_ref)) if last else (qt_ref[h + 1], (kl_ref, kc_ref))
        m_next, acc = None, None
        for chunk in chunks:
            part, c0, c1, r0 = chunk
            m_next = scores(q_next, keys_next, (h + 1) % 2, chunk, m_next)
            p = jnp.exp2(s_scr[h % 2, r0:r0 + c1 - c0, :] - m).astype(BF16)
            pv = _mm(values[part][:, c0:c1], p)
            acc = pv if acc is None else acc + pv
        o = acc[0:HEAD_DIM, :] / acc[HEAD_DIM:HEAD_DIM + 1, :]
        o_ref[:, h * HEAD_DIM:(h + 1) * HEAD_DIM] = o.T.astype(BF16)
        m = m_next
    m_scr[...] = m


def _attention(qt, k_parts, vt_parts):
    bsz, heads, _, t = qt.shape
    kvh = vt_parts[0].shape[1]
    l = sum(k.shape[1] for k in k_parts)
    q_per_kv = heads // kvh
    assert q_per_kv % 2 == 0, "score buffers alternate per head and must line up across grid steps"
    tq = _tiles()["attn_q"]
    nq = t // tq
    steps = bsz * kvh * nq

    def following(b, j, i):
        lin = jnp.minimum((b * kvh + j) * nq + i + 1, steps - 1)
        return lin // (kvh * nq), (lin // nq) % kvh, lin % nq

    def q_next_map(b, j, i):
        b2, j2, i2 = following(b, j, i)
        return b2, j2 * q_per_kv, 0, i2

    def k_next_map(b, j, i):
        b2, j2, _ = following(b, j, i)
        return b2, 0, j2

    return pl.pallas_call(
        _attn_kernel,
        grid=(bsz, kvh, nq),
        in_specs=[pl.BlockSpec((None, q_per_kv, HEAD_DIM, tq), lambda b, j, i: (b, j, 0, i)),
                  pl.BlockSpec((None, None, HEAD_DIM, tq), q_next_map)]
                 + [pl.BlockSpec((None, k.shape[1], HEAD_DIM), lambda b, j, i: (b, 0, j)) for k in k_parts]
                 + [pl.BlockSpec((None, k.shape[1], HEAD_DIM), k_next_map) for k in k_parts]
                 + [pl.BlockSpec((None, None, V_ROWS, v.shape[3]), lambda b, j, i: (b, j, 0, 0)) for v in vt_parts],
        out_specs=pl.BlockSpec((None, tq, q_per_kv * HEAD_DIM), lambda b, j, i: (b, i, j)),
        out_shape=jax.ShapeDtypeStruct((bsz, t, heads * HEAD_DIM), BF16),
        scratch_shapes=[pltpu.VMEM((2, l, tq), F32), pltpu.VMEM((1, tq), F32)],
        compiler_params=_params(("arbitrary",) * 3),
        name="attention",
    )(qt, qt, *k_parts, *k_parts, *vt_parts)


def kernel(x, c, ctx, c_ctx, w_mod, b_mod, norm_pre, norm_post, ffn_w_in, ffn_w_out, ab_w_in, ab_w_out, sgu_norm_g, sgu_w, sgu_b, s5_lam_re, s5_lam_im, s5_log_step, s5_b_re, s5_b_im, s5_c_re, s5_c_im, s5_d, s5_glu_w, s5_glu_b, attn_w_qkv, attn_w_out, attn_q_norm, attn_k_norm):
    bsz, seq, d = x.shape
    n_ctx = ctx.shape[1]
    depth = w_mod.shape[0]
    assert bsz == V7X_SUBLANES, "the S5 scan keeps the batch on the sublane axis"
    d_a = sgu_norm_g.shape[1]
    d_b = s5_d.shape[1]
    kv_dim = (attn_w_qkv.shape[2] - d) // 2
    q_per_kv = d // kv_dim

    def lat_row(i, tm):
        return (i * tm) // seq

    def ctx_row(i, tm):
        return CTX_MOD_ROW

    cond = jnp.zeros((N_MOD_ROWS, d), F32).at[:bsz].set(c).at[CTX_MOD_ROW].set(c_ctx)
    mod_all = _modulation(cond, w_mod, b_mod).reshape(depth, N_MOD_ROWS, -1, d)
    mod_all = jnp.pad(mod_all, ((0, 0), (0, 0), (0, N_MOD_ROWS - mod_all.shape[2]), (0, 0)))

    xl = x.reshape(bsz * seq, d)
    xc = ctx.reshape(bsz * n_ctx, d)
    w_in_all, w_out_all = ffn_w_in.astype(BF16), ffn_w_out.astype(BF16)

    for i in range(depth):
        last = i == depth - 1
        j = i // 2
        mod = mod_all[i]
        gains = [jnp.zeros((V7X_SUBLANES, d), F32).at[0].set(norm_pre[i, s]).at[1].set(norm_post[i, s])
                 for s in range(3)]
        w_in1, w_in2 = (w_in_all, (i, 0)), (w_in_all, (i, 1))
        w_out1, w_out2 = (w_out_all, (i, 0)), (w_out_all, (i, 1))

        xl = _ffn(xl, mod, gains[0], w_in1, w_out1, 0, lat_row)
        xc = _ffn(xc, mod, gains[0], w_in1, w_out1, 0, ctx_row)

        if i % 2 == 0:
            w_in = ab_w_in[j].astype(BF16)
            norm_g = sgu_norm_g[j].reshape(1, d_a)
            ug_l, vn_l, us_l = _mix0_in(xl, mod, gains[1], w_in, norm_g, _tiles()["mix"], lat_row, "mix0_in")
            ug_c, vn_c, us_c = _mix0_in(xc, mod, gains[1], w_in, norm_g, n_ctx, ctx_row, "mix0_in_ctx")
            b_mats, c_mats, lam = _s5_operands(s5_lam_re[j], s5_lam_im[j], s5_log_step[j], s5_b_re[j],
                                               s5_b_im[j], s5_c_re[j], s5_c_im[j])
            h_zero = jnp.zeros((2, bsz, b_mats.shape[-1]), F32)
            *y_c, h_ctx = _s5_scan(us_c.reshape(bsz, n_ctx, d_b), h_zero, b_mats, c_mats, lam, "s5_scan_ctx")
            *y_l, _ = _s5_scan(us_l.reshape(bsz, seq, d_b), h_ctx, b_mats, c_mats, lam, "s5_scan")
            consts = (sgu_w[j].astype(BF16),
                      jnp.broadcast_to(sgu_b[j][:, :, None], sgu_w[j].shape).astype(F32),
                      s5_d[j].reshape(1, d_b), s5_glu_w[j].astype(BF16), s5_glu_b[j].reshape(1, d_b),
                      ab_w_out[j].astype(BF16))
            mixer = "sgu_s5"
            mix_l = (ug_l, vn_l) + tuple(y.reshape(bsz * seq, d_b) for y in y_l) + (us_l,)
            mix_c = (ug_c, vn_c) + tuple(y.reshape(bsz * n_ctx, d_b) for y in y_c) + (us_c,)
            ctx_tile = (n_ctx, 2)
        else:
            if not last:
                raise NotImplementedError("context stream through an attention layer")
            w_qkv = attn_w_qkv[j].astype(BF16)
            qk_gains = jnp.zeros((V7X_SUBLANES, HEAD_DIM), F32).at[0].set(attn_q_norm[j]).at[1].set(attn_k_norm[j])
            rope_tab = _rope_tables(seq)
            qt, k, vt = _qkv(xl, xc, mod, gains[1], w_qkv, qk_gains, rope_tab, bsz)
            mixer = "attn"
            mix_l, mix_c = (_attention(qt, k, vt).reshape(bsz * seq, d),), None
            consts = (attn_w_out[j].astype(BF16),)
            ctx_tile = None

        gains2 = gains[2].at[2].set(norm_post[i, 1])
        xl = _ffn(xl, mod, gains2, w_in2, w_out2, 2, lat_row, mixer, mix_l, consts)
        if not last:
            xc = _ffn(xc, mod, gains2, w_in2, w_out2, 2, ctx_row, mixer, mix_c, consts, ctx_tile)

    return xl.reshape(bsz, seq, d)
```

```python
import functools
import math

import jax
import jax.numpy as jnp
from jax import lax
from jax.experimental import pallas as pl
from jax.experimental.pallas import tpu as pltpu

F32 = jnp.float32
BF16 = jnp.bfloat16

NORM_EPS = 1e-6
FFN_RES = 0.5
GRID_W = 64
ROPE_THETA = 10000.0
SGU_CHUNK = 128
HEAD_DIM = 128
ATTN_KEY_CHUNK = 256
V_ONES_ROWS = 16
V_ROWS = HEAD_DIM + V_ONES_ROWS

V7X_LANES = 128
V7X_SUBLANES = 8
MXU_COLS = 256
VMEM_LIMIT = 56 * 1024 * 1024

N_MOD_ROWS = 16
CTX_MOD_ROW = 8


def _tiles():
    return dict(ffn=(1024, 4), ffn_attn=(1024, 4), ffn_sgu_s5=(512, 2), mix=(1024, 4), mix_ctx_sub=2,
                attn_q=512, s5_steps=128, mod_n=2304)


def _params(sem, vmem=VMEM_LIMIT):
    return pltpu.CompilerParams(dimension_semantics=sem, vmem_limit_bytes=vmem)


def _const_spec(shape):
    nd = len(shape)
    return pl.BlockSpec(shape, lambda *_: (0,) * nd)


def _resident_spec(shape, lead=()):
    nd = len(shape) - len(lead)
    return pl.BlockSpec((None,) * len(lead) + tuple(shape[len(lead):]),
                        lambda *_: tuple(lead) + (0,) * nd, pipeline_mode=pl.Buffered(1))


def _rms(x):
    return x * lax.rsqrt(jnp.mean(x * x, axis=-1, keepdims=True) + NORM_EPS)


def _mm(a, b):
    return jnp.dot(a, b, preferred_element_type=F32)


def _mod_kernel(c_ref, w_ref, b_ref, o_ref):
    c = c_ref[...]
    s = (c * jax.nn.sigmoid(c)).astype(BF16)
    o_ref[...] = _mm(s, w_ref[...].astype(BF16)) + b_ref[...]


def _modulation(cond, w_mod, b_mod):
    depth, d, n = w_mod.shape
    tn = _tiles()["mod_n"]
    return pl.pallas_call(
        _mod_kernel,
        grid=(depth, n // tn),
        in_specs=[_const_spec((N_MOD_ROWS, d)),
                  pl.BlockSpec((None, d, tn), lambda l, j: (l, 0, j)),
                  pl.BlockSpec((None, 1, tn), lambda l, j: (l, 0, j))],
        out_specs=pl.BlockSpec((None, N_MOD_ROWS, tn), lambda l, j: (l, 0, j)),
        out_shape=jax.ShapeDtypeStruct((depth, N_MOD_ROWS, n), F32),
        compiler_params=_params(("parallel", "parallel")),
        name="modulation",
    )(cond, w_mod, b_mod.reshape(depth, 1, n))


def _ffn_chunks(d_ff, width=512):
    return [(c, min(c + width, d_ff)) for c in range(0, d_ff, width)]


def _attn_mix_out(r, x_ref, mod_ref, g_ref, mix_refs, z_scr):
    a_ref, wo_ref = mix_refs
    y = _mm(a_ref[r, :], wo_ref[...])
    return x_ref[r, :] + mod_ref[5:6, :] * (_rms(y) * g_ref[2:3, :])


def _sgu_s5_mix_out(r, x_ref, mod_ref, g_ref, mix_refs, z_scr):
    ug_ref, vn_ref, yf_ref, yb_ref, us_ref, sw_ref, sb_ref, d_ref, gw_ref, gb_ref, wo_ref = mix_refs
    d_a = ug_ref.shape[1]
    row_chunks = range(r.start, r.stop, SGU_CHUNK)
    for g in range(d_a // SGU_CHUNK):
        c0 = g * SGU_CHUNK
        vn_wide = jnp.concatenate([vn_ref[r0:r0 + SGU_CHUNK, c0:c0 + SGU_CHUNK] for r0 in row_chunks], axis=1)
        mixed_wide = _mm(sw_ref[g], vn_wide)
        for n, r0 in enumerate(row_chunks):
            mixed = mixed_wide[:, n * SGU_CHUNK:(n + 1) * SGU_CHUNK] + sb_ref[g]
            z_scr[n * SGU_CHUNK:(n + 1) * SGU_CHUNK, c0:c0 + SGU_CHUNK] = (
                ug_ref[r0:r0 + SGU_CHUNK, c0:c0 + SGU_CHUNK].astype(F32) * mixed).astype(BF16)
    ys = jax.nn.gelu(yf_ref[r, :] + yb_ref[r, :] + d_ref[...] * us_ref[r, :])
    gl = jax.nn.sigmoid(_mm(ys.astype(BF16), gw_ref[...]) + gb_ref[...])
    z_scr[:, d_a:] = (ys * gl).astype(BF16)
    y = _mm(z_scr[...], wo_ref[...])
    return x_ref[r, :] + mod_ref[5:6, :] * (_rms(y) * g_ref[2:3, :])


_MIX_OUT = {"attn": (_attn_mix_out, 2, False), "sgu_s5": (_sgu_s5_mix_out, 11, True)}


def _ffn_kernel(sub, n_sub, mixer, x_ref, mod_ref, g_ref, win_ref, wout_ref, *rest):
    mix_fn, n_mix, needs_z = _MIX_OUT[mixer] if mixer else (None, 0, False)
    mix_refs, rest = rest[:n_mix], rest[n_mix:]
    o_ref, h_scr, a_scr, y_scr = rest[:4]
    z_scr = rest[4] if needs_z else None
    d_ff = wout_ref.shape[0]
    rows_per = x_ref.shape[0] // n_sub
    shift, scale, gate = (mod_ref[3 * sub + k:3 * sub + k + 1, :] for k in range(3))
    pre_gain = g_ref[0:1, :] * (1.0 + scale)
    post_gain = (FFN_RES * gate) * g_ref[1:2, :]

    def rows(s):
        return slice(s * rows_per, (s + 1) * rows_per)

    def prologue(s):
        if mix_fn is None:
            x = x_ref[rows(s), :]
        else:
            x = mix_fn(rows(s), x_ref, mod_ref, g_ref, mix_refs, z_scr.at[s % 2] if needs_z else None)
            o_ref[rows(s), :] = x
        h_scr[s % 2] = (_rms(x) * pre_gain + shift).astype(BF16)

    def up_chunk(s, c0, c1):
        h = h_scr[s % 2]
        g = _mm(h, win_ref[:, c0:c1])
        u = _mm(h, win_ref[:, d_ff + c0:d_ff + c1])
        a_scr[s % 2, :, c0:c1] = (g * jax.nn.sigmoid(g) * u).astype(BF16)

    def epilogue(s):
        base = x_ref if mix_fn is None else o_ref
        o_ref[rows(s), :] = base[rows(s), :] + post_gain * _rms(y_scr[s % 2])

    prologue(0)
    for s in range(n_sub):
        for ci, (c0, c1) in enumerate(_ffn_chunks(d_ff)):
            up_chunk(s, c0, c1)
            if ci == 0 and s > 0:
                epilogue(s - 1)
            if ci == 2 and s + 1 < n_sub:
                prologue(s + 1)
        y_scr[s % 2] = _mm(a_scr[s % 2], wout_ref[...])
    epilogue(n_sub - 1)


def _ffn(x, mod, gains, w_in, w_out, sub, row_of, mixer=None, mix_rows=(), mix_consts=(), tile=None):
    r, d = x.shape
    (w_in, in_lead), (w_out, out_lead) = w_in, w_out
    d_ff = w_out.shape[-2]
    tm, n_sub = tile or _tiles()[f"ffn_{mixer}" if mixer else "ffn"]

    row_spec = lambda a: pl.BlockSpec((tm, a.shape[1]), lambda i: (i, 0))
    sub_rows = tm // n_sub
    scratch = [pltpu.VMEM((2, sub_rows, d), BF16), pltpu.VMEM((2, sub_rows, d_ff), BF16),
               pltpu.VMEM((2, sub_rows, d), F32)]
    if mixer and _MIX_OUT[mixer][2]:
        scratch.append(pltpu.VMEM((2, sub_rows, d), BF16))
    return pl.pallas_call(
        functools.partial(_ffn_kernel, sub, n_sub, mixer),
        grid=(r // tm,),
        in_specs=[row_spec(x),
                  pl.BlockSpec((None, N_MOD_ROWS, d), lambda i: (row_of(i, tm), 0, 0)),
                  _const_spec(gains.shape),
                  _resident_spec(w_in.shape, in_lead),
                  _resident_spec(w_out.shape, out_lead)]
                 + [row_spec(a) for a in mix_rows] + [_resident_spec(c.shape) for c in mix_consts],
        out_specs=row_spec(x),
        out_shape=jax.ShapeDtypeStruct((r, d), F32),
        scratch_shapes=scratch,
        compiler_params=_params(("parallel",)),
        name=f"ffn{sub}" + (f"_{mixer}" if mixer else ""),
    )(x, mod, gains, w_in, w_out, *mix_rows, *mix_consts)


def _mix0_in_kernel(n_sub, x_ref, mod_ref, g_ref, w_ref, ng_ref, ug_ref, vn_ref, us_ref, h_scr):
    d_a = ug_ref.shape[1]
    rows_per = x_ref.shape[0] // n_sub

    def rows(s):
        return slice(s * rows_per, (s + 1) * rows_per)

    def prologue(s):
        h_scr[rows(s), :] = _modulated(x_ref[rows(s), :], mod_ref, g_ref)

    def project(s):
        r = rows(s)
        h = h_scr[r, :]
        for c0 in range(0, d_a, MXU_COLS):
            ug_ref[r, c0:c0 + MXU_COLS] = jax.nn.gelu(_mm(h, w_ref[:, c0:c0 + MXU_COLS])).astype(BF16)
            vv = jax.nn.gelu(_mm(h, w_ref[:, d_a + c0:d_a + c0 + MXU_COLS]))
            for g0 in range(0, MXU_COLS, SGU_CHUNK):
                v = vv[:, g0:g0 + SGU_CHUNK]
                mu = jnp.mean(v, axis=-1, keepdims=True)
                vc = v - mu
                var = jnp.mean(vc * vc, axis=-1, keepdims=True)
                vn_ref[r, c0 + g0:c0 + g0 + SGU_CHUNK] = (
                    vc * lax.rsqrt(var + NORM_EPS) * ng_ref[:, c0 + g0:c0 + g0 + SGU_CHUNK]).astype(BF16)
        us_ref[r, :] = _mm(h, w_ref[:, 2 * d_a:])

    prologue(0)
    for s in range(n_sub):
        if s + 1 < n_sub:
            prologue(s + 1)
        project(s)


def _mix0_in(x, mod, gains, w_in, norm_g, tile, row_of, name):
    r, d = x.shape
    d_a = norm_g.shape[1]
    d_b = w_in.shape[1] - 2 * d_a
    tm, n_sub = tile
    rows = lambda w: pl.BlockSpec((tm, w), lambda i: (i, 0))
    return pl.pallas_call(
        functools.partial(_mix0_in_kernel, n_sub),
        grid=(r // tm,),
        in_specs=[rows(d),
                  pl.BlockSpec((None, N_MOD_ROWS, d), lambda i: (row_of(i, tm), 0, 0)),
                  _const_spec(gains.shape), _const_spec(w_in.shape), _const_spec(norm_g.shape)],
        out_specs=[rows(d_a), rows(d_a), rows(d_b)],
        out_shape=[jax.ShapeDtypeStruct((r, d_a), BF16)] * 2 + [jax.ShapeDtypeStruct((r, d_b), F32)],
        scratch_shapes=[pltpu.VMEM((tm, d), BF16)],
        compiler_params=_params(("parallel",)),
        name=name,
    )(x, mod, gains, w_in, norm_g)


def _s5_kernel(steps, uf_ref, ub_ref, h0_ref, bf_ref, bb_ref, cf_ref, cb_ref, lam_ref,
               yf_ref, yb_ref, h_ref, sf_scr, sb_scr, hf_scr, hb_scr, r_scr, t_scr):
    nb = V7X_SUBLANES
    n = lam_ref.shape[1]
    lane_slabs = [slice(c, c + V7X_LANES) for c in range(0, uf_ref.shape[2], V7X_LANES)]
    pitch = r_scr.shape[2] // nb

    def batch_rows(b):
        return slice(b * pitch, b * pitch + steps)

    def time_rows(t):
        return slice(t * nb, (t + 1) * nb)

    def to_time_major(use, u_ref):
        for b in range(nb):
            for sl, lanes in enumerate(lane_slabs):
                r_scr[use, sl, batch_rows(b), :] = u_ref[b, :, lanes]
        for t in range(steps):
            for sl, lanes in enumerate(lane_slabs):
                t_scr[use, time_rows(t), lanes] = r_scr[use, sl, pl.ds(t, nb, stride=pitch), :]
        return t_scr[use]

    def from_time_major(use, y, y_ref):
        t_scr[use] = y
        for t in range(steps):
            for sl, lanes in enumerate(lane_slabs):
                r_scr[use, sl, pl.ds(t, nb, stride=pitch), :] = t_scr[use, time_rows(t), lanes]
        for b in range(nb):
            for sl, lanes in enumerate(lane_slabs):
                y_ref[b, :, lanes] = r_scr[use, sl, batch_rows(b), :]

    @pl.when(pl.program_id(0) == 0)
    def _():
        hf_scr[...] = h0_ref[0]
        hb_scr[...] = h0_ref[1]

    sf_scr[...] = _mm(to_time_major(0, uf_ref).astype(BF16), bf_ref[...])
    sb_scr[...] = _mm(to_time_major(1, ub_ref).astype(BF16), bb_ref[...])

    def scan(s_scr, h_scr, lam_row, reverse):
        lr = lam_ref[lam_row:lam_row + nb, :]
        li = lam_ref[lam_row + nb:lam_row + 2 * nb, :]
        hr, hi = h_scr[:, 0:n], h_scr[:, n:2 * n]
        for k in range(steps):
            t = (steps - 1 - k) if reverse else k
            rows = slice(t * nb, (t + 1) * nb)
            hr, hi = (lr * hr - li * hi + s_scr[rows, 0:n],
                      lr * hi + li * hr + s_scr[rows, n:2 * n])
            s_scr[rows, 0:n] = hr
            s_scr[rows, n:2 * n] = hi
        h_scr[:, 0:n] = hr
        h_scr[:, n:2 * n] = hi

    scan(sf_scr, hf_scr, 0, False)
    yf = _mm(sf_scr[...].astype(BF16), cf_ref[...])
    scan(sb_scr, hb_scr, 2 * nb, True)
    yb = _mm(sb_scr[...].astype(BF16), cb_ref[...])
    from_time_major(2, yf, yf_ref)
    from_time_major(3, yb, yb_ref)
    h_ref[0] = hf_scr[...]
    h_ref[1] = hb_scr[...]


def _s5_scan(us, h0, b_mats, c_mats, lam, name):
    bsz, t, d_b = us.shape
    nb = V7X_SUBLANES
    steps = _tiles()["s5_steps"]
    n_chunks = t // steps
    n2 = b_mats.shape[-1]
    pitch = steps + nb
    n_slabs = d_b // V7X_LANES
    fwd = pl.BlockSpec((bsz, steps, d_b), lambda k: (0, k, 0))
    bwd = pl.BlockSpec((bsz, steps, d_b), lambda k: (0, n_chunks - 1 - k, 0))
    return pl.pallas_call(
        functools.partial(_s5_kernel, steps),
        grid=(n_chunks,),
        in_specs=[fwd, bwd, _const_spec(h0.shape),
                  pl.BlockSpec((None, d_b, n2), lambda k: (0, 0, 0)),
                  pl.BlockSpec((None, d_b, n2), lambda k: (1, 0, 0)),
                  pl.BlockSpec((None, n2, d_b), lambda k: (0, 0, 0)),
                  pl.BlockSpec((None, n2, d_b), lambda k: (1, 0, 0)),
                  _const_spec(lam.shape)],
        out_specs=[fwd, bwd, _const_spec(h0.shape)],
        out_shape=[jax.ShapeDtypeStruct(us.shape, F32)] * 2 + [jax.ShapeDtypeStruct(h0.shape, F32)],
        scratch_shapes=[pltpu.VMEM((steps * nb, n2), F32), pltpu.VMEM((steps * nb, n2), F32),
                        pltpu.VMEM((nb, n2), F32), pltpu.VMEM((nb, n2), F32),
                        pltpu.VMEM((4, n_slabs, nb * pitch, V7X_LANES), F32),
                        pltpu.VMEM((4, steps * nb, d_b), F32)],
        compiler_params=_params(("arbitrary",)),
        name=name,
    )(us, us, h0, b_mats, b_mats, c_mats, c_mats, lam)


def _s5_operands(lam_re, lam_im, log_step, b_re, b_im, c_re, c_im):
    n_dir, groups, states = lam_re.shape
    gdim = b_re.shape[-1]
    dt = jnp.exp(log_step.astype(F32))[..., None]
    lr, li = lam_re.astype(F32), lam_im.astype(F32)
    mag = jnp.exp(lr * dt)
    ar, ai = mag * jnp.cos(li * dt), mag * jnp.sin(li * dt)
    den = lr * lr + li * li
    fr = ((ar - 1.0) * lr + ai * li) / den
    fi = (ai * lr - (ar - 1.0) * li) / den
    bbr = fr[..., None] * b_re - fi[..., None] * b_im
    bbi = fr[..., None] * b_im + fi[..., None] * b_re
    eye = jnp.eye(groups, dtype=F32)
    n = groups * states

    def in_mat(b):
        return jnp.einsum("dgpc,gh->dgchp", b, eye).reshape(n_dir, groups * gdim, n)

    def out_mat(c):
        return jnp.einsum("dgcp,gh->dgphc", c, eye).reshape(n_dir, n, groups * gdim)

    b_mats = jnp.concatenate([in_mat(bbr), in_mat(bbi)], axis=-1).astype(BF16)
    c_mats = jnp.concatenate([out_mat(c_re.astype(F32)), -out_mat(c_im.astype(F32))], axis=1).astype(BF16)
    nb = V7X_SUBLANES
    lam = jnp.concatenate([jnp.broadcast_to(v.reshape(1, n), (nb, n))
                           for v in (ar[0], ai[0], ar[1], ai[1])], axis=0)
    return b_mats, c_mats, lam


def _rope(x, cos, sin_lo, sin_hi):
    quarter = HEAD_DIM // 4
    return (x * cos + pltpu.roll(x, HEAD_DIM - quarter, axis=1) * sin_lo
            + pltpu.roll(x, quarter, axis=1) * sin_hi)


def _modulated(x, mod_ref, g_ref):
    return (_rms(x) * (g_ref[0:1, :] * (1.0 + mod_ref[4:5, :])) + mod_ref[3:4, :]).astype(BF16)


def _head_slices(h, w_ref, col0, n_heads):
    per_dot = MXU_COLS // HEAD_DIM
    assert n_heads % per_dot == 0
    for pair in range(n_heads // per_dot):
        c0 = col0 + pair * MXU_COLS
        wide = _mm(h, w_ref[:, c0:c0 + MXU_COLS])
        for sub in range(per_dot):
            yield wide[:, sub * HEAD_DIM:(sub + 1) * HEAD_DIM]


def _kv_heads(h, r, w_ref, ng_ref, q_dim, k_ref, vt_ref, rope):
    kvh = vt_ref.shape[0]
    for hd, k in enumerate(_head_slices(h, w_ref, q_dim, kvh)):
        k = _rms(k) * ng_ref[1:2, :]
        if rope is not None:
            k = _rope(k, *rope)
        k_ref[r, hd * HEAD_DIM:(hd + 1) * HEAD_DIM] = k.astype(BF16)
    for hd, v in enumerate(_head_slices(h, w_ref, q_dim + kvh * HEAD_DIM, kvh)):
        vt_ref[hd, 0:HEAD_DIM, r] = v.T.astype(BF16)
        vt_ref[hd, HEAD_DIM:, r] = jnp.ones((V_ONES_ROWS, r.stop - r.start), BF16)


def _kv_ctx_kernel(x_ref, mod_ref, g_ref, w_ref, ng_ref, k_ref, vt_ref):
    q_dim = w_ref.shape[1] - 2 * k_ref.shape[1]
    _kv_heads(_modulated(x_ref[...], mod_ref, g_ref), slice(0, x_ref.shape[0]), w_ref, ng_ref, q_dim,
              k_ref, vt_ref, None)


def _qkv_lat_kernel(n_sub, x_ref, mod_ref, g_ref, w_ref, ng_ref, rope_ref, qtab_ref,
                    qt_ref, k_ref, vt_ref, h_scr):
    heads = qt_ref.shape[0]
    rows_per = x_ref.shape[0] // n_sub
    quarter = HEAD_DIM // 4

    def rows(s):
        return slice(s * rows_per, (s + 1) * rows_per)

    def prologue(s):
        h_scr[rows(s), :] = _modulated(x_ref[rows(s), :], mod_ref, g_ref)

    def project(s):
        r = rows(s)
        h = h_scr[r, :]
        for hd, p in enumerate(_head_slices(h, w_ref, 0, heads)):
            pt = p.T
            inv = lax.rsqrt(jnp.mean(pt * pt, axis=0, keepdims=True) + NORM_EPS)
            partner = jnp.concatenate([pt[quarter:2 * quarter], pt[0:quarter],
                                       pt[3 * quarter:], pt[2 * quarter:3 * quarter]], axis=0)
            qt_ref[hd, :, r] = ((pt * qtab_ref[0, :, r] + partner * qtab_ref[1, :, r]) * inv).astype(BF16)
        rope = tuple(rope_ref[i, r, :] for i in range(3))
        _kv_heads(h, r, w_ref, ng_ref, heads * HEAD_DIM, k_ref, vt_ref, rope)

    prologue(0)
    for s in range(n_sub):
        if s + 1 < n_sub:
            prologue(s + 1)
        project(s)


def _qkv(xl, xc, mod, gains, w_qkv, qk_gains, rope_tab, bsz):
    d = xl.shape[1]
    seq, n_ctx = xl.shape[0] // bsz, xc.shape[0] // bsz
    kv_dim = (w_qkv.shape[1] - d) // 2
    heads, kvh = d // HEAD_DIM, kv_dim // HEAD_DIM
    tm, n_sub = _tiles()["mix"]
    pos_blocks = seq // tm

    def kv_shapes(n):
        return [jax.ShapeDtypeStruct((bsz, n, kv_dim), BF16), jax.ShapeDtypeStruct((bsz, kvh, V_ROWS, n), BF16)]

    consts = [_const_spec(gains.shape), _const_spec(w_qkv.shape), _const_spec(qk_gains.shape)]
    k_ctx, vt_ctx = pl.pallas_call(
        _kv_ctx_kernel,
        grid=(bsz,),
        in_specs=[pl.BlockSpec((n_ctx, d), lambda b: (b, 0)),
                  pl.BlockSpec((None, N_MOD_ROWS, d), lambda b: (CTX_MOD_ROW, 0, 0))] + consts,
        out_specs=[pl.BlockSpec((None, n_ctx, kv_dim), lambda b: (b, 0, 0)),
                   pl.BlockSpec((None, kvh, V_ROWS, n_ctx), lambda b: (b, 0, 0, 0))],
        out_shape=kv_shapes(n_ctx),
        compiler_params=_params(("parallel",)),
        name="kv_ctx",
    )(xc, mod, gains, w_qkv, qk_gains)
    qt, k_lat, vt_lat = pl.pallas_call(
        functools.partial(_qkv_lat_kernel, n_sub),
        grid=(bsz * pos_blocks,),
        in_specs=[pl.BlockSpec((tm, d), lambda i: (i, 0)),
                  pl.BlockSpec((None, N_MOD_ROWS, d), lambda i: (i // pos_blocks, 0, 0))] + consts
                 + [pl.BlockSpec((3, tm, HEAD_DIM), lambda i: (0, i % pos_blocks, 0)),
                    pl.BlockSpec((2, HEAD_DIM, tm), lambda i: (0, 0, i % pos_blocks))],
        out_specs=[pl.BlockSpec((None, heads, HEAD_DIM, tm), lambda i: (i // pos_blocks, 0, 0, i % pos_blocks)),
                   pl.BlockSpec((None, tm, kv_dim), lambda i: (i // pos_blocks, i % pos_blocks, 0)),
                   pl.BlockSpec((None, kvh, V_ROWS, tm), lambda i: (i // pos_blocks, 0, 0, i % pos_blocks))],
        out_shape=[jax.ShapeDtypeStruct((bsz, heads, HEAD_DIM, seq), BF16)] + kv_shapes(seq),
        scratch_shapes=[pltpu.VMEM((tm, d), BF16)],
        compiler_params=_params(("parallel",)),
        name="qkv",
    )(xl, mod, gains, w_qkv, qk_gains, rope_tab, _q_rope_tables(rope_tab, qk_gains[0]))
    return qt, (k_lat, k_ctx), (vt_lat, vt_ctx)


def _q_rope_tables(rope_tab, q_gain):
    quarter = HEAD_DIM // 4
    partner = jnp.concatenate([jnp.arange(quarter, 2 * quarter), jnp.arange(0, quarter),
                               jnp.arange(3 * quarter, 4 * quarter), jnp.arange(2 * quarter, 3 * quarter)])
    scale = HEAD_DIM ** -0.5 * math.log2(math.e)
    cos_g = rope_tab[0] * (q_gain * scale)[None, :]
    sin_g = (rope_tab[1] + rope_tab[2]) * (q_gain[partner] * scale)[None, :]
    return jnp.stack([cos_g.T, sin_g.T])


def _rope_tables(seq):
    rows = seq // GRID_W
    axis_dim = HEAD_DIM // 2
    quarter = axis_dim // 2
    row_id = jnp.repeat(jnp.arange(rows, dtype=F32), GRID_W)
    col_id = jnp.tile(jnp.arange(GRID_W, dtype=F32), rows)
    inv_freq = ROPE_THETA ** (-jnp.arange(0, axis_dim, 2, dtype=F32) / axis_dim)
    a_row, a_col = row_id[:, None] * inv_freq, col_id[:, None] * inv_freq
    zero = jnp.zeros((seq, quarter), F32)
    cos = jnp.concatenate([jnp.cos(a_row)] * 2 + [jnp.cos(a_col)] * 2, axis=1)
    sin_lo = jnp.concatenate([-jnp.sin(a_row), zero, -jnp.sin(a_col), zero], axis=1)
    sin_hi = jnp.concatenate([zero, jnp.sin(a_row), zero, jnp.sin(a_col)], axis=1)
    return jnp.stack([cos, sin_lo, sin_hi])


def _attn_kernel(qt_ref, qn_ref, kl_ref, kc_ref, kln_ref, kcn_ref, vtl_ref, vtc_ref, o_ref, s_scr, m_scr):
    heads = qt_ref.shape[0]
    n_lat = kl_ref.shape[0]
    chunks = [(part, c, min(c + ATTN_KEY_CHUNK, n), base + c)
              for part, n, base in ((0, n_lat, 0), (1, kc_ref.shape[0], n_lat))
              for c in range(0, n, ATTN_KEY_CHUNK)]
    values = (vtl_ref, vtc_ref)

    def scores(q, keys, slot, chunk, m):
        part, c0, c1, r0 = chunk
        s = _mm(keys[part][c0:c1, :], q)
        s_scr[slot, r0:r0 + c1 - c0, :] = s
        cm = jnp.max(s, axis=0, keepdims=True)
        return cm if m is None else jnp.maximum(m, cm)

    @pl.when((pl.program_id(0) == 0) & (pl.program_id(1) == 0) & (pl.program_id(2) == 0))
    def _():
        m0 = None
        for chunk in chunks:
            m0 = scores(qt_ref[0], (kl_ref, kc_ref), 0, chunk, m0)
        m_scr[...] = m0

    m = m_scr[...]
    for h in range(heads):
        last = h + 1 == heads
        q_next, keys_next = (qn_ref[...], (kln_ref, kcn_ref)) if last else (qt_ref[h + 1], (kl_ref, kc_ref))
        m_next, acc = None, None
        for chunk in chunks:
            part, c0, c1, r0 = chunk
            m_next = scores(q_next, keys_next, (h + 1) % 2, chunk, m_next)
            p = jnp.exp2(s_scr[h % 2, r0:r0 + c1 - c0, :] - m).astype(BF16)
            pv = _mm(values[part][:, c0:c1], p)
            acc = pv if acc is None else acc + pv
        o = acc[0:HEAD_DIM, :] / acc[HEAD_DIM:HEAD_DIM + 1, :]
        o_ref[:, h * HEAD_DIM:(h + 1) * HEAD_DIM] = o.T.astype(BF16)
        m = m_next
    m_scr[...] = m


def _attention(qt, k_parts, vt_parts):
    bsz, heads, _, t = qt.shape
    kvh = vt_parts[0].shape[1]
    l = sum(k.shape[1] for k in k_parts)
    q_per_kv = heads // kvh
    assert q_per_kv % 2 == 0, "score buffers alternate per head and must line up across grid steps"
    tq = _tiles()["attn_q"]
    nq = t // tq
    steps = bsz * kvh * nq

    def following(b, j, i):
        lin = jnp.minimum((b * kvh + j) * nq + i + 1, steps - 1)
        return lin // (kvh * nq), (lin // nq) % kvh, lin % nq

    def q_next_map(b, j, i):
        b2, j2, i2 = following(b, j, i)
        return b2, j2 * q_per_kv, 0, i2

    def k_next_map(b, j, i):
        b2, j2, _ = following(b, j, i)
        return b2, 0, j2

    return pl.pallas_call(
        _attn_kernel,
        grid=(bsz, kvh, nq),
        in_specs=[pl.BlockSpec((None, q_per_kv, HEAD_DIM, tq), lambda b, j, i: (b, j, 0, i)),
                  pl.BlockSpec((None, None, HEAD_DIM, tq), q_next_map)]
                 + [pl.BlockSpec((None, k.shape[1], HEAD_DIM), lambda b, j, i: (b, 0, j)) for k in k_parts]
                 + [pl.BlockSpec((None, k.shape[1], HEAD_DIM), k_next_map) for k in k_parts]
                 + [pl.BlockSpec((None, None, V_ROWS, v.shape[3]), lambda b, j, i: (b, j, 0, 0)) for v in vt_parts],
        out_specs=pl.BlockSpec((None, tq, q_per_kv * HEAD_DIM), lambda b, j, i: (b, i, j)),
        out_shape=jax.ShapeDtypeStruct((bsz, t, heads * HEAD_DIM), BF16),
        scratch_shapes=[pltpu.VMEM((2, l, tq), F32), pltpu.VMEM((1, tq), F32)],
        compiler_params=_params(("arbitrary",) * 3),
        name="attention",
    )(qt, qt, *k_parts, *k_parts, *vt_parts)


def kernel(x, c, ctx, c_ctx, w_mod, b_mod, norm_pre, norm_post, ffn_w_in, ffn_w_out, ab_w_in, ab_w_out, sgu_norm_g, sgu_w, sgu_b, s5_lam_re, s5_lam_im, s5_log_step, s5_b_re, s5_b_im, s5_c_re, s5_c_im, s5_d, s5_glu_w, s5_glu_b, attn_w_qkv, attn_w_out, attn_q_norm, attn_k_norm):
    bsz, seq, d = x.shape
    n_ctx = ctx.shape[1]
    depth = w_mod.shape[0]
    assert bsz == V7X_SUBLANES, "the S5 scan keeps the batch on the sublane axis"
    d_a = sgu_norm_g.shape[1]
    d_b = s5_d.shape[1]

    def lat_row(i, tm):
        return (i * tm) // seq

    def ctx_row(i, tm):
        return CTX_MOD_ROW

    cond = jnp.zeros((N_MOD_ROWS, d), F32).at[:bsz].set(c).at[CTX_MOD_ROW].set(c_ctx)
    mod_all = _modulation(cond, w_mod, b_mod).reshape(depth, N_MOD_ROWS, -1, d)
    mod_all = jnp.pad(mod_all, ((0, 0), (0, 0), (0, N_MOD_ROWS - mod_all.shape[2]), (0, 0)))

    xl = x.reshape(bsz * seq, d)
    xc = ctx.reshape(bsz * n_ctx, d)
    w_in_all, w_out_all = ffn_w_in.astype(BF16), ffn_w_out.astype(BF16)

    for i in range(depth):
        last = i == depth - 1
        j = i // 2
        mod = mod_all[i]
        gains = [jnp.zeros((V7X_SUBLANES, d), F32).at[0].set(norm_pre[i, s]).at[1].set(norm_post[i, s])
                 for s in range(3)]
        w_in1, w_in2 = (w_in_all, (i, 0)), (w_in_all, (i, 1))
        w_out1, w_out2 = (w_out_all, (i, 0)), (w_out_all, (i, 1))

        xl = _ffn(xl, mod, gains[0], w_in1, w_out1, 0, lat_row)
        xc = _ffn(xc, mod, gains[0], w_in1, w_out1, 0, ctx_row)

        if i % 2 == 0:
            w_in = ab_w_in[j].astype(BF16)
            norm_g = sgu_norm_g[j].reshape(1, d_a)
            ug_l, vn_l, us_l = _mix0_in(xl, mod, gains[1], w_in, norm_g, _tiles()["mix"], lat_row, "mix0_in")
            ug_c, vn_c, us_c = _mix0_in(xc, mod, gains[1], w_in, norm_g, (n_ctx, _tiles()["mix_ctx_sub"]),
                                        ctx_row, "mix0_in_ctx")
            b_mats, c_mats, lam = _s5_operands(s5_lam_re[j], s5_lam_im[j], s5_log_step[j], s5_b_re[j],
                                               s5_b_im[j], s5_c_re[j], s5_c_im[j])
            h_zero = jnp.zeros((2, bsz, b_mats.shape[-1]), F32)
            *y_c, h_ctx = _s5_scan(us_c.reshape(bsz, n_ctx, d_b), h_zero, b_mats, c_mats, lam, "s5_scan_ctx")
            *y_l, _ = _s5_scan(us_l.reshape(bsz, seq, d_b), h_ctx, b_mats, c_mats, lam, "s5_scan")
            consts = (sgu_w[j].astype(BF16),
                      jnp.broadcast_to(sgu_b[j][:, :, None], sgu_w[j].shape).astype(F32),
                      s5_d[j].reshape(1, d_b), s5_glu_w[j].astype(BF16), s5_glu_b[j].reshape(1, d_b),
                      ab_w_out[j].astype(BF16))
            mixer = "sgu_s5"
            mix_l = (ug_l, vn_l) + tuple(y.reshape(bsz * seq, d_b) for y in y_l) + (us_l,)
            mix_c = (ug_c, vn_c) + tuple(y.reshape(bsz * n_ctx, d_b) for y in y_c) + (us_c,)
            ctx_tile = (n_ctx, 2)
        else:
            if not last:
                raise NotImplementedError("context stream through an attention layer")
            w_qkv = attn_w_qkv[j].astype(BF16)
            qk_gains = jnp.zeros((V7X_SUBLANES, HEAD_DIM), F32).at[0].set(attn_q_norm[j]).at[1].set(attn_k_norm[j])
            rope_tab = _rope_tables(seq)
            qt, k, vt = _qkv(xl, xc, mod, gains[1], w_qkv, qk_gains, rope_tab, bsz)
            mixer = "attn"
            mix_l, mix_c = (_attention(qt, k, vt).reshape(bsz * seq, d),), None
            consts = (attn_w_out[j].astype(BF16),)
            ctx_tile = None

        gains2 = gains[2].at[2].set(norm_post[i, 1])
        xl = _ffn(xl, mod, gains2, w_in2, w_out2, 2, lat_row, mixer, mix_l, consts)
        if not last:
            xc = _ffn(xc, mod, gains2, w_in2, w_out2, 2, ctx_row, mixer, mix_c, consts, ctx_tile)

    return xl.reshape(bsz, seq, d)
```

```python
import functools
import math

import jax
import jax.numpy as jnp
from jax import lax
from jax.experimental import pallas as pl
from jax.experimental.pallas import tpu as pltpu

F32 = jnp.float32
BF16 = jnp.bfloat16

NORM_EPS = 1e-6
FFN_RES = 0.5
GRID_W = 64
ROPE_THETA = 10000.0
SGU_CHUNK = 128
HEAD_DIM = 128
ATTN_KEY_CHUNK = 256
V_ONES_ROWS = 16
V_ROWS = HEAD_DIM + V_ONES_ROWS

V7X_LANES = 128
V7X_SUBLANES = 8
MXU_COLS = 256
VMEM_LIMIT = 56 * 1024 * 1024

N_MOD_ROWS = 16
CTX_MOD_ROW = 8


def _tiles():
    return dict(ffn=(1024, 4), ffn_attn=(1024, 4), ffn_sgu_s5=(512, 2), mix=(1024, 4), mix_ctx_sub=2,
                attn_q=512, attn_kv_per_step=2, s5_steps=128, mod_n=2304)


def _params(sem, vmem=VMEM_LIMIT):
    return pltpu.CompilerParams(dimension_semantics=sem, vmem_limit_bytes=vmem)


def _const_spec(shape):
    nd = len(shape)
    return pl.BlockSpec(shape, lambda *_: (0,) * nd)


def _resident_spec(shape, lead=()):
    nd = len(shape) - len(lead)
    return pl.BlockSpec((None,) * len(lead) + tuple(shape[len(lead):]),
                        lambda *_: tuple(lead) + (0,) * nd, pipeline_mode=pl.Buffered(1))


def _rms(x):
    return x * lax.rsqrt(jnp.mean(x * x, axis=-1, keepdims=True) + NORM_EPS)


def _mm(a, b):
    return jnp.dot(a, b, preferred_element_type=F32)


def _mod_kernel(c_ref, w_ref, b_ref, o_ref):
    c = c_ref[...]
    s = (c * jax.nn.sigmoid(c)).astype(BF16)
    o_ref[...] = _mm(s, w_ref[...].astype(BF16)) + b_ref[...]


def _modulation(cond, w_mod, b_mod):
    depth, d, n = w_mod.shape
    tn = _tiles()["mod_n"]
    return pl.pallas_call(
        _mod_kernel,
        grid=(depth, n // tn),
        in_specs=[_const_spec((N_MOD_ROWS, d)),
                  pl.BlockSpec((None, d, tn), lambda l, j: (l, 0, j)),
                  pl.BlockSpec((None, 1, tn), lambda l, j: (l, 0, j))],
        out_specs=pl.BlockSpec((None, N_MOD_ROWS, tn), lambda l, j: (l, 0, j)),
        out_shape=jax.ShapeDtypeStruct((depth, N_MOD_ROWS, n), F32),
        compiler_params=_params(("parallel", "parallel")),
        name="modulation",
    )(cond, w_mod, b_mod.reshape(depth, 1, n))


def _ffn_chunks(d_ff, width=512):
    return [(c, min(c + width, d_ff)) for c in range(0, d_ff, width)]


def _attn_mix_out(r, x_ref, mod_ref, g_ref, mix_refs, z_scr):
    a_ref, wo_ref = mix_refs
    y = _mm(a_ref[r, :], wo_ref[...])
    return x_ref[r, :] + mod_ref[5:6, :] * (_rms(y) * g_ref[2:3, :])


def _sgu_s5_mix_out(r, x_ref, mod_ref, g_ref, mix_refs, z_scr):
    ug_ref, vn_ref, yf_ref, yb_ref, us_ref, sw_ref, sb_ref, d_ref, gw_ref, gb_ref, wo_ref = mix_refs
    d_a = ug_ref.shape[1]
    row_chunks = range(r.start, r.stop, SGU_CHUNK)
    for g in range(d_a // SGU_CHUNK):
        c0 = g * SGU_CHUNK
        vn_wide = jnp.concatenate([vn_ref[r0:r0 + SGU_CHUNK, c0:c0 + SGU_CHUNK] for r0 in row_chunks], axis=1)
        mixed_wide = _mm(sw_ref[g], vn_wide)
        for n, r0 in enumerate(row_chunks):
            mixed = mixed_wide[:, n * SGU_CHUNK:(n + 1) * SGU_CHUNK] + sb_ref[g]
            z_scr[n * SGU_CHUNK:(n + 1) * SGU_CHUNK, c0:c0 + SGU_CHUNK] = (
                ug_ref[r0:r0 + SGU_CHUNK, c0:c0 + SGU_CHUNK].astype(F32) * mixed).astype(BF16)
    ys = jax.nn.gelu(yf_ref[r, :] + yb_ref[r, :] + d_ref[...] * us_ref[r, :])
    gl = jax.nn.sigmoid(_mm(ys.astype(BF16), gw_ref[...]) + gb_ref[...])
    z_scr[:, d_a:] = (ys * gl).astype(BF16)
    y = _mm(z_scr[...], wo_ref[...])
    return x_ref[r, :] + mod_ref[5:6, :] * (_rms(y) * g_ref[2:3, :])


_MIX_OUT = {"attn": (_attn_mix_out, 2, False), "sgu_s5": (_sgu_s5_mix_out, 11, True)}


def _ffn_kernel(sub, n_sub, mixer, x_ref, mod_ref, g_ref, win_ref, wout_ref, *rest):
    mix_fn, n_mix, needs_z = _MIX_OUT[mixer] if mixer else (None, 0, False)
    mix_refs, rest = rest[:n_mix], rest[n_mix:]
    o_ref, h_scr, a_scr, y_scr = rest[:4]
    z_scr = rest[4] if needs_z else None
    d_ff = wout_ref.shape[0]
    rows_per = x_ref.shape[0] // n_sub
    shift, scale, gate = (mod_ref[3 * sub + k:3 * sub + k + 1, :] for k in range(3))
    pre_gain = g_ref[0:1, :] * (1.0 + scale)
    post_gain = (FFN_RES * gate) * g_ref[1:2, :]

    def rows(s):
        return slice(s * rows_per, (s + 1) * rows_per)

    def prologue(s):
        if mix_fn is None:
            x = x_ref[rows(s), :]
        else:
            x = mix_fn(rows(s), x_ref, mod_ref, g_ref, mix_refs, z_scr.at[s % 2] if needs_z else None)
            o_ref[rows(s), :] = x
        h_scr[s % 2] = (_rms(x) * pre_gain + shift).astype(BF16)

    def up_chunk(s, c0, c1):
        h = h_scr[s % 2]
        g = _mm(h, win_ref[:, c0:c1])
        u = _mm(h, win_ref[:, d_ff + c0:d_ff + c1])
        a_scr[s % 2, :, c0:c1] = (g * jax.nn.sigmoid(g) * u).astype(BF16)

    def epilogue(s):
        base = x_ref if mix_fn is None else o_ref
        o_ref[rows(s), :] = base[rows(s), :] + post_gain * _rms(y_scr[s % 2])

    prologue(0)
    for s in range(n_sub):
        for ci, (c0, c1) in enumerate(_ffn_chunks(d_ff)):
            up_chunk(s, c0, c1)
            if ci == 0 and s > 0:
                epilogue(s - 1)
            if ci == 2 and s + 1 < n_sub:
                prologue(s + 1)
        y_scr[s % 2] = _mm(a_scr[s % 2], wout_ref[...])
    epilogue(n_sub - 1)


def _ffn(x, mod, gains, w_in, w_out, sub, row_of, mixer=None, mix_rows=(), mix_consts=(), tile=None):
    r, d = x.shape
    (w_in, in_lead), (w_out, out_lead) = w_in, w_out
    d_ff = w_out.shape[-2]
    tm, n_sub = tile or _tiles()[f"ffn_{mixer}" if mixer else "ffn"]

    row_spec = lambda a: pl.BlockSpec((tm, a.shape[1]), lambda i: (i, 0))
    sub_rows = tm // n_sub
    scratch = [pltpu.VMEM((2, sub_rows, d), BF16), pltpu.VMEM((2, sub_rows, d_ff), BF16),
               pltpu.VMEM((2, sub_rows, d), F32)]
    if mixer and _MIX_OUT[mixer][2]:
        scratch.append(pltpu.VMEM((2, sub_rows, d), BF16))
    return pl.pallas_call(
        functools.partial(_ffn_kernel, sub, n_sub, mixer),
        grid=(r // tm,),
        in_specs=[row_spec(x),
                  pl.BlockSpec((None, N_MOD_ROWS, d), lambda i: (row_of(i, tm), 0, 0)),
                  _const_spec(gains.shape),
                  _resident_spec(w_in.shape, in_lead),
                  _resident_spec(w_out.shape, out_lead)]
                 + [row_spec(a) for a in mix_rows] + [_resident_spec(c.shape) for c in mix_consts],
        out_specs=row_spec(x),
        out_shape=jax.ShapeDtypeStruct((r, d), F32),
        scratch_shapes=scratch,
        compiler_params=_params(("parallel",)),
        name=f"ffn{sub}" + (f"_{mixer}" if mixer else ""),
    )(x, mod, gains, w_in, w_out, *mix_rows, *mix_consts)


def _mix0_in_kernel(n_sub, x_ref, mod_ref, g_ref, w_ref, ng_ref, ug_ref, vn_ref, us_ref, h_scr):
    d_a = ug_ref.shape[1]
    rows_per = x_ref.shape[0] // n_sub

    def rows(s):
        return slice(s * rows_per, (s + 1) * rows_per)

    def prologue(s):
        h_scr[rows(s), :] = _modulated(x_ref[rows(s), :], mod_ref, g_ref)

    def project(s):
        r = rows(s)
        h = h_scr[r, :]
        for c0 in range(0, d_a, MXU_COLS):
            ug_ref[r, c0:c0 + MXU_COLS] = jax.nn.gelu(_mm(h, w_ref[:, c0:c0 + MXU_COLS])).astype(BF16)
            vv = jax.nn.gelu(_mm(h, w_ref[:, d_a + c0:d_a + c0 + MXU_COLS]))
            for g0 in range(0, MXU_COLS, SGU_CHUNK):
                v = vv[:, g0:g0 + SGU_CHUNK]
                mu = jnp.mean(v, axis=-1, keepdims=True)
                vc = v - mu
                var = jnp.mean(vc * vc, axis=-1, keepdims=True)
                vn_ref[r, c0 + g0:c0 + g0 + SGU_CHUNK] = (
                    vc * lax.rsqrt(var + NORM_EPS) * ng_ref[:, c0 + g0:c0 + g0 + SGU_CHUNK]).astype(BF16)
        us_ref[r, :] = _mm(h, w_ref[:, 2 * d_a:])

    prologue(0)
    for s in range(n_sub):
        if s + 1 < n_sub:
            prologue(s + 1)
        project(s)


def _mix0_in(x, mod, gains, w_in, norm_g, tile, row_of, name):
    r, d = x.shape
    d_a = norm_g.shape[1]
    d_b = w_in.shape[1] - 2 * d_a
    tm, n_sub = tile
    rows = lambda w: pl.BlockSpec((tm, w), lambda i: (i, 0))
    return pl.pallas_call(
        functools.partial(_mix0_in_kernel, n_sub),
        grid=(r // tm,),
        in_specs=[rows(d),
                  pl.BlockSpec((None, N_MOD_ROWS, d), lambda i: (row_of(i, tm), 0, 0)),
                  _const_spec(gains.shape), _const_spec(w_in.shape), _const_spec(norm_g.shape)],
        out_specs=[rows(d_a), rows(d_a), rows(d_b)],
        out_shape=[jax.ShapeDtypeStruct((r, d_a), BF16)] * 2 + [jax.ShapeDtypeStruct((r, d_b), F32)],
        scratch_shapes=[pltpu.VMEM((tm, d), BF16)],
        compiler_params=_params(("parallel",)),
        name=name,
    )(x, mod, gains, w_in, norm_g)


def _s5_kernel(steps, uf_ref, ub_ref, h0_ref, bf_ref, bb_ref, cf_ref, cb_ref, lam_ref,
               yf_ref, yb_ref, h_ref, sf_scr, sb_scr, hf_scr, hb_scr, r_scr, t_scr):
    nb = V7X_SUBLANES
    n = lam_ref.shape[1]
    lane_slabs = [slice(c, c + V7X_LANES) for c in range(0, uf_ref.shape[2], V7X_LANES)]
    pitch = r_scr.shape[2] // nb

    def batch_rows(b):
        return slice(b * pitch, b * pitch + steps)

    def time_rows(t):
        return slice(t * nb, (t + 1) * nb)

    def to_time_major(use, u_ref):
        for b in range(nb):
            for sl, lanes in enumerate(lane_slabs):
                r_scr[use, sl, batch_rows(b), :] = u_ref[b, :, lanes]
        for t in range(steps):
            for sl, lanes in enumerate(lane_slabs):
                t_scr[use, time_rows(t), lanes] = r_scr[use, sl, pl.ds(t, nb, stride=pitch), :]
        return t_scr[use]

    def from_time_major(use, y, y_ref):
        t_scr[use] = y
        for t in range(steps):
            for sl, lanes in enumerate(lane_slabs):
                r_scr[use, sl, pl.ds(t, nb, stride=pitch), :] = t_scr[use, time_rows(t), lanes]
        for b in range(nb):
            for sl, lanes in enumerate(lane_slabs):
                y_ref[b, :, lanes] = r_scr[use, sl, batch_rows(b), :]

    @pl.when(pl.program_id(0) == 0)
    def _():
        hf_scr[...] = h0_ref[0]
        hb_scr[...] = h0_ref[1]

    sf_scr[...] = _mm(to_time_major(0, uf_ref).astype(BF16), bf_ref[...])
    sb_scr[...] = _mm(to_time_major(1, ub_ref).astype(BF16), bb_ref[...])

    def scan(s_scr, h_scr, lam_row, reverse):
        lr = lam_ref[lam_row:lam_row + nb, :]
        li = lam_ref[lam_row + nb:lam_row + 2 * nb, :]
        hr, hi = h_scr[:, 0:n], h_scr[:, n:2 * n]
        for k in range(steps):
            t = (steps - 1 - k) if reverse else k
            rows = slice(t * nb, (t + 1) * nb)
            hr, hi = (lr * hr - li * hi + s_scr[rows, 0:n],
                      lr * hi + li * hr + s_scr[rows, n:2 * n])
            s_scr[rows, 0:n] = hr
            s_scr[rows, n:2 * n] = hi
        h_scr[:, 0:n] = hr
        h_scr[:, n:2 * n] = hi

    scan(sf_scr, hf_scr, 0, False)
    yf = _mm(sf_scr[...].astype(BF16), cf_ref[...])
    scan(sb_scr, hb_scr, 2 * nb, True)
    yb = _mm(sb_scr[...].astype(BF16), cb_ref[...])
    from_time_major(2, yf, yf_ref)
    from_time_major(3, yb, yb_ref)
    h_ref[0] = hf_scr[...]
    h_ref[1] = hb_scr[...]


def _s5_scan(us, h0, b_mats, c_mats, lam, name):
    bsz, t, d_b = us.shape
    nb = V7X_SUBLANES
    steps = _tiles()["s5_steps"]
    n_chunks = t // steps
    n2 = b_mats.shape[-1]
    pitch = steps + nb
    n_slabs = d_b // V7X_LANES
    fwd = pl.BlockSpec((bsz, steps, d_b), lambda k: (0, k, 0))
    bwd = pl.BlockSpec((bsz, steps, d_b), lambda k: (0, n_chunks - 1 - k, 0))
    return pl.pallas_call(
        functools.partial(_s5_kernel, steps),
        grid=(n_chunks,),
        in_specs=[fwd, bwd, _const_spec(h0.shape),
                  pl.BlockSpec((None, d_b, n2), lambda k: (0, 0, 0)),
                  pl.BlockSpec((None, d_b, n2), lambda k: (1, 0, 0)),
                  pl.BlockSpec((None, n2, d_b), lambda k: (0, 0, 0)),
                  pl.BlockSpec((None, n2, d_b), lambda k: (1, 0, 0)),
                  _const_spec(lam.shape)],
        out_specs=[fwd, bwd, _const_spec(h0.shape)],
        out_shape=[jax.ShapeDtypeStruct(us.shape, F32)] * 2 + [jax.ShapeDtypeStruct(h0.shape, F32)],
        scratch_shapes=[pltpu.VMEM((steps * nb, n2), F32), pltpu.VMEM((steps * nb, n2), F32),
                        pltpu.VMEM((nb, n2), F32), pltpu.VMEM((nb, n2), F32),
                        pltpu.VMEM((4, n_slabs, nb * pitch, V7X_LANES), F32),
                        pltpu.VMEM((4, steps * nb, d_b), F32)],
        compiler_params=_params(("arbitrary",)),
        name=name,
    )(us, us, h0, b_mats, b_mats, c_mats, c_mats, lam)


def _s5_operands(lam_re, lam_im, log_step, b_re, b_im, c_re, c_im):
    n_dir, groups, states = lam_re.shape
    gdim = b_re.shape[-1]
    dt = jnp.exp(log_step.astype(F32))[..., None]
    lr, li = lam_re.astype(F32), lam_im.astype(F32)
    mag = jnp.exp(lr * dt)
    ar, ai = mag * jnp.cos(li * dt), mag * jnp.sin(li * dt)
    den = lr * lr + li * li
    fr = ((ar - 1.0) * lr + ai * li) / den
    fi = (ai * lr - (ar - 1.0) * li) / den
    bbr = fr[..., None] * b_re - fi[..., None] * b_im
    bbi = fr[..., None] * b_im + fi[..., None] * b_re
    eye = jnp.eye(groups, dtype=F32)
    n = groups * states

    def in_mat(b):
        return jnp.einsum("dgpc,gh->dgchp", b, eye).reshape(n_dir, groups * gdim, n)

    def out_mat(c):
        return jnp.einsum("dgcp,gh->dgphc", c, eye).reshape(n_dir, n, groups * gdim)

    b_mats = jnp.concatenate([in_mat(bbr), in_mat(bbi)], axis=-1).astype(BF16)
    c_mats = jnp.concatenate([out_mat(c_re.astype(F32)), -out_mat(c_im.astype(F32))], axis=1).astype(BF16)
    nb = V7X_SUBLANES
    lam = jnp.concatenate([jnp.broadcast_to(v.reshape(1, n), (nb, n))
                           for v in (ar[0], ai[0], ar[1], ai[1])], axis=0)
    return b_mats, c_mats, lam


def _rope(x, cos, sin_lo, sin_hi):
    quarter = HEAD_DIM // 4
    return (x * cos + pltpu.roll(x, HEAD_DIM - quarter, axis=1) * sin_lo
            + pltpu.roll(x, quarter, axis=1) * sin_hi)


def _modulated(x, mod_ref, g_ref):
    return (_rms(x) * (g_ref[0:1, :] * (1.0 + mod_ref[4:5, :])) + mod_ref[3:4, :]).astype(BF16)


def _head_slices(h, w_ref, col0, n_heads):
    per_dot = MXU_COLS // HEAD_DIM
    assert n_heads % per_dot == 0
    for pair in range(n_heads // per_dot):
        c0 = col0 + pair * MXU_COLS
        wide = _mm(h, w_ref[:, c0:c0 + MXU_COLS])
        for sub in range(per_dot):
            yield wide[:, sub * HEAD_DIM:(sub + 1) * HEAD_DIM]


def _kv_heads(h, r, w_ref, ng_ref, q_dim, k_ref, vt_ref, rope):
    kvh = vt_ref.shape[0]
    for hd, k in enumerate(_head_slices(h, w_ref, q_dim, kvh)):
        k = _rms(k) * ng_ref[1:2, :]
        if rope is not None:
            k = _rope(k, *rope)
        k_ref[r, hd * HEAD_DIM:(hd + 1) * HEAD_DIM] = k.astype(BF16)
    for hd, v in enumerate(_head_slices(h, w_ref, q_dim + kvh * HEAD_DIM, kvh)):
        vt_ref[hd, 0:HEAD_DIM, r] = v.T.astype(BF16)
        vt_ref[hd, HEAD_DIM:, r] = jnp.ones((V_ONES_ROWS, r.stop - r.start), BF16)


def _kv_ctx_kernel(x_ref, mod_ref, g_ref, w_ref, ng_ref, k_ref, vt_ref):
    q_dim = w_ref.shape[1] - 2 * k_ref.shape[1]
    _kv_heads(_modulated(x_ref[...], mod_ref, g_ref), slice(0, x_ref.shape[0]), w_ref, ng_ref, q_dim,
              k_ref, vt_ref, None)


def _qkv_lat_kernel(n_sub, x_ref, mod_ref, g_ref, w_ref, ng_ref, rope_ref, qtab_ref,
                    qt_ref, k_ref, vt_ref, h_scr):
    heads = qt_ref.shape[0]
    rows_per = x_ref.shape[0] // n_sub
    quarter = HEAD_DIM // 4

    def rows(s):
        return slice(s * rows_per, (s + 1) * rows_per)

    def prologue(s):
        h_scr[rows(s), :] = _modulated(x_ref[rows(s), :], mod_ref, g_ref)

    def project(s):
        r = rows(s)
        h = h_scr[r, :]
        for hd, p in enumerate(_head_slices(h, w_ref, 0, heads)):
            pt = p.T
            inv = lax.rsqrt(jnp.mean(pt * pt, axis=0, keepdims=True) + NORM_EPS)
            partner = jnp.concatenate([pt[quarter:2 * quarter], pt[0:quarter],
                                       pt[3 * quarter:], pt[2 * quarter:3 * quarter]], axis=0)
            qt_ref[hd, :, r] = ((pt * qtab_ref[0, :, r] + partner * qtab_ref[1, :, r]) * inv).astype(BF16)
        rope = tuple(rope_ref[i, r, :] for i in range(3))
        _kv_heads(h, r, w_ref, ng_ref, heads * HEAD_DIM, k_ref, vt_ref, rope)

    prologue(0)
    for s in range(n_sub):
        if s + 1 < n_sub:
            prologue(s + 1)
        project(s)


def _qkv(xl, xc, mod, gains, w_qkv, qk_gains, rope_tab, bsz):
    d = xl.shape[1]
    seq, n_ctx = xl.shape[0] // bsz, xc.shape[0] // bsz
    kv_dim = (w_qkv.shape[1] - d) // 2
    heads, kvh = d // HEAD_DIM, kv_dim // HEAD_DIM
    tm, n_sub = _tiles()["mix"]
    pos_blocks = seq // tm

    def kv_shapes(n):
        return [jax.ShapeDtypeStruct((bsz, n, kv_dim), BF16), jax.ShapeDtypeStruct((bsz, kvh, V_ROWS, n), BF16)]

    consts = [_const_spec(gains.shape), _const_spec(w_qkv.shape), _const_spec(qk_gains.shape)]
    k_ctx, vt_ctx = pl.pallas_call(
        _kv_ctx_kernel,
        grid=(bsz,),
        in_specs=[pl.BlockSpec((n_ctx, d), lambda b: (b, 0)),
                  pl.BlockSpec((None, N_MOD_ROWS, d), lambda b: (CTX_MOD_ROW, 0, 0))] + consts,
        out_specs=[pl.BlockSpec((None, n_ctx, kv_dim), lambda b: (b, 0, 0)),
                   pl.BlockSpec((None, kvh, V_ROWS, n_ctx), lambda b: (b, 0, 0, 0))],
        out_shape=kv_shapes(n_ctx),
        compiler_params=_params(("parallel",)),
        name="kv_ctx",
    )(xc, mod, gains, w_qkv, qk_gains)
    qt, k_lat, vt_lat = pl.pallas_call(
        functools.partial(_qkv_lat_kernel, n_sub),
        grid=(bsz * pos_blocks,),
        in_specs=[pl.BlockSpec((tm, d), lambda i: (i, 0)),
                  pl.BlockSpec((None, N_MOD_ROWS, d), lambda i: (i // pos_blocks, 0, 0))] + consts
                 + [pl.BlockSpec((3, tm, HEAD_DIM), lambda i: (0, i % pos_blocks, 0)),
                    pl.BlockSpec((2, HEAD_DIM, tm), lambda i: (0, 0, i % pos_blocks))],
        out_specs=[pl.BlockSpec((None, heads, HEAD_DIM, tm), lambda i: (i // pos_blocks, 0, 0, i % pos_blocks)),
                   pl.BlockSpec((None, tm, kv_dim), lambda i: (i // pos_blocks, i % pos_blocks, 0)),
                   pl.BlockSpec((None, kvh, V_ROWS, tm), lambda i: (i // pos_blocks, 0, 0, i % pos_blocks))],
        out_shape=[jax.ShapeDtypeStruct((bsz, heads, HEAD_DIM, seq), BF16)] + kv_shapes(seq),
        scratch_shapes=[pltpu.VMEM((tm, d), BF16)],
        compiler_params=_params(("parallel",)),
        name="qkv",
    )(xl, mod, gains, w_qkv, qk_gains, rope_tab, _q_rope_tables(rope_tab, qk_gains[0]))
    return qt, (k_lat, k_ctx), (vt_lat, vt_ctx)


def _q_rope_tables(rope_tab, q_gain):
    quarter = HEAD_DIM // 4
    partner = jnp.concatenate([jnp.arange(quarter, 2 * quarter), jnp.arange(0, quarter),
                               jnp.arange(3 * quarter, 4 * quarter), jnp.arange(2 * quarter, 3 * quarter)])
    scale = HEAD_DIM ** -0.5 * math.log2(math.e)
    cos_g = rope_tab[0] * (q_gain * scale)[None, :]
    sin_g = (rope_tab[1] + rope_tab[2]) * (q_gain[partner] * scale)[None, :]
    return jnp.stack([cos_g.T, sin_g.T])


def _rope_tables(seq):
    rows = seq // GRID_W
    axis_dim = HEAD_DIM // 2
    quarter = axis_dim // 2
    row_id = jnp.repeat(jnp.arange(rows, dtype=F32), GRID_W)
    col_id = jnp.tile(jnp.arange(GRID_W, dtype=F32), rows)
    inv_freq = ROPE_THETA ** (-jnp.arange(0, axis_dim, 2, dtype=F32) / axis_dim)
    a_row, a_col = row_id[:, None] * inv_freq, col_id[:, None] * inv_freq
    zero = jnp.zeros((seq, quarter), F32)
    cos = jnp.concatenate([jnp.cos(a_row)] * 2 + [jnp.cos(a_col)] * 2, axis=1)
    sin_lo = jnp.concatenate([-jnp.sin(a_row), zero, -jnp.sin(a_col), zero], axis=1)
    sin_hi = jnp.concatenate([zero, jnp.sin(a_row), zero, jnp.sin(a_col)], axis=1)
    return jnp.stack([cos, sin_lo, sin_hi])


def _attn_kernel(q_per_kv, qt_ref, qn_ref, kl_ref, kc_ref, kln_ref, kcn_ref, vtl_ref, vtc_ref, o_ref,
                 s_scr, m_scr):
    heads = qt_ref.shape[0]
    n_lat = kl_ref.shape[0]
    chunks = [(part, c, min(c + ATTN_KEY_CHUNK, n), base + c)
              for part, n, base in ((0, n_lat, 0), (1, kc_ref.shape[0], n_lat))
              for c in range(0, n, ATTN_KEY_CHUNK)]
    values = (vtl_ref, vtc_ref)

    def scores(q, keys, g, slot, chunk, m):
        part, c0, c1, r0 = chunk
        s = _mm(keys[part][c0:c1, g * HEAD_DIM:(g + 1) * HEAD_DIM], q)
        s_scr[slot, r0:r0 + c1 - c0, :] = s
        cm = jnp.max(s, axis=0, keepdims=True)
        return cm if m is None else jnp.maximum(m, cm)

    @pl.when((pl.program_id(0) == 0) & (pl.program_id(1) == 0) & (pl.program_id(2) == 0))
    def _():
        m0 = None
        for chunk in chunks:
            m0 = scores(qt_ref[0], (kl_ref, kc_ref), 0, 0, chunk, m0)
        m_scr[...] = m0

    m = m_scr[...]
    for h in range(heads):
        g = h // q_per_kv
        if h + 1 == heads:
            q_next, keys_next, g_next = qn_ref[...], (kln_ref, kcn_ref), 0
        else:
            q_next, keys_next, g_next = qt_ref[h + 1], (kl_ref, kc_ref), (h + 1) // q_per_kv
        m_next, acc = None, None
        for chunk in chunks:
            part, c0, c1, r0 = chunk
            m_next = scores(q_next, keys_next, g_next, (h + 1) % 2, chunk, m_next)
            p = jnp.exp2(s_scr[h % 2, r0:r0 + c1 - c0, :] - m).astype(BF16)
            pv = _mm(values[part][g, :, c0:c1], p)
            acc = pv if acc is None else acc + pv
        o = acc[0:HEAD_DIM, :] / acc[HEAD_DIM:HEAD_DIM + 1, :]
        o_ref[:, h * HEAD_DIM:(h + 1) * HEAD_DIM] = o.T.astype(BF16)
        m = m_next
    m_scr[...] = m


def _attention(qt, k_parts, vt_parts):
    bsz, heads, _, t = qt.shape
    kvh = vt_parts[0].shape[1]
    l = sum(k.shape[1] for k in k_parts)
    q_per_kv = heads // kvh
    assert q_per_kv % 2 == 0, "score buffers alternate per head and must line up across grid steps"
    tq, kv_step = _tiles()["attn_q"], _tiles()["attn_kv_per_step"]
    nq = t // tq
    groups = kvh // kv_step
    step_heads = kv_step * q_per_kv
    steps = bsz * groups * nq

    def following(b, j, i):
        lin = jnp.minimum((b * groups + j) * nq + i + 1, steps - 1)
        return lin // (groups * nq), (lin // nq) % groups, lin % nq

    def q_next_map(b, j, i):
        b2, j2, i2 = following(b, j, i)
        return b2, j2 * step_heads, 0, i2

    def k_next_map(b, j, i):
        b2, j2, _ = following(b, j, i)
        return b2, 0, j2

    return pl.pallas_call(
        functools.partial(_attn_kernel, q_per_kv),
        grid=(bsz, groups, nq),
        in_specs=[pl.BlockSpec((None, step_heads, HEAD_DIM, tq), lambda b, j, i: (b, j, 0, i)),
                  pl.BlockSpec((None, None, HEAD_DIM, tq), q_next_map)]
                 + [pl.BlockSpec((None, k.shape[1], kv_step * HEAD_DIM), lambda b, j, i: (b, 0, j)) for k in k_parts]
                 + [pl.BlockSpec((None, k.shape[1], kv_step * HEAD_DIM), k_next_map) for k in k_parts]
                 + [pl.BlockSpec((None, kv_step, V_ROWS, v.shape[3]), lambda b, j, i: (b, j, 0, 0)) for v in vt_parts],
        out_specs=pl.BlockSpec((None, tq, step_heads * HEAD_DIM), lambda b, j, i: (b, i, j)),
        out_shape=jax.ShapeDtypeStruct((bsz, t, heads * HEAD_DIM), BF16),
        scratch_shapes=[pltpu.VMEM((2, l, tq), F32), pltpu.VMEM((1, tq), F32)],
        compiler_params=_params(("arbitrary",) * 3),
        name="attention",
    )(qt, qt, *k_parts, *k_parts, *vt_parts)


def kernel(x, c, ctx, c_ctx, w_mod, b_mod, norm_pre, norm_post, ffn_w_in, ffn_w_out, ab_w_in, ab_w_out, sgu_norm_g, sgu_w, sgu_b, s5_lam_re, s5_lam_im, s5_log_step, s5_b_re, s5_b_im, s5_c_re, s5_c_im, s5_d, s5_glu_w, s5_glu_b, attn_w_qkv, attn_w_out, attn_q_norm, attn_k_norm):
    bsz, seq, d = x.shape
    n_ctx = ctx.shape[1]
    depth = w_mod.shape[0]
    assert bsz == V7X_SUBLANES, "the S5 scan keeps the batch on the sublane axis"
    d_a = sgu_norm_g.shape[1]
    d_b = s5_d.shape[1]

    def lat_row(i, tm):
        return (i * tm) // seq

    def ctx_row(i, tm):
        return CTX_MOD_ROW

    cond = jnp.zeros((N_MOD_ROWS, d), F32).at[:bsz].set(c).at[CTX_MOD_ROW].set(c_ctx)
    mod_all = _modulation(cond, w_mod, b_mod).reshape(depth, N_MOD_ROWS, -1, d)
    mod_all = jnp.pad(mod_all, ((0, 0), (0, 0), (0, N_MOD_ROWS - mod_all.shape[2]), (0, 0)))

    xl = x.reshape(bsz * seq, d)
    xc = ctx.reshape(bsz * n_ctx, d)
    w_in_all, w_out_all = ffn_w_in.astype(BF16), ffn_w_out.astype(BF16)

    for i in range(depth):
        last = i == depth - 1
        j = i // 2
        mod = mod_all[i]
        gains = [jnp.zeros((V7X_SUBLANES, d), F32).at[0].set(norm_pre[i, s]).at[1].set(norm_post[i, s])
                 for s in range(3)]
        w_in1, w_in2 = (w_in_all, (i, 0)), (w_in_all, (i, 1))
        w_out1, w_out2 = (w_out_all, (i, 0)), (w_out_all, (i, 1))

        xl = _ffn(xl, mod, gains[0], w_in1, w_out1, 0, lat_row)
        xc = _ffn(xc, mod, gains[0], w_in1, w_out1, 0, ctx_row)

        if i % 2 == 0:
            w_in = ab_w_in[j].astype(BF16)
            norm_g = sgu_norm_g[j].reshape(1, d_a)
            ug_l, vn_l, us_l = _mix0_in(xl, mod, gains[1], w_in, norm_g, _tiles()["mix"], lat_row, "mix0_in")
            ug_c, vn_c, us_c = _mix0_in(xc, mod, gains[1], w_in, norm_g, (n_ctx, _tiles()["mix_ctx_sub"]),
                                        ctx_row, "mix0_in_ctx")
            b_mats, c_mats, lam = _s5_operands(s5_lam_re[j], s5_lam_im[j], s5_log_step[j], s5_b_re[j],
                                               s5_b_im[j], s5_c_re[j], s5_c_im[j])
            h_zero = jnp.zeros((2, bsz, b_mats.shape[-1]), F32)
            *y_c, h_ctx = _s5_scan(us_c.reshape(bsz, n_ctx, d_b), h_zero, b_mats, c_mats, lam, "s5_scan_ctx")
            *y_l, _ = _s5_scan(us_l.reshape(bsz, seq, d_b), h_ctx, b_mats, c_mats, lam, "s5_scan")
            consts = (sgu_w[j].astype(BF16),
                      jnp.broadcast_to(sgu_b[j][:, :, None], sgu_w[j].shape).astype(F32),
                      s5_d[j].reshape(1, d_b), s5_glu_w[j].astype(BF16), s5_glu_b[j].reshape(1, d_b),
                      ab_w_out[j].astype(BF16))
            mixer = "sgu_s5"
            mix_l = (ug_l, vn_l) + tuple(y.reshape(bsz * seq, d_b) for y in y_l) + (us_l,)
            mix_c = (ug_c, vn_c) + tuple(y.reshape(bsz * n_ctx, d_b) for y in y_c) + (us_c,)
            ctx_tile = (n_ctx, 2)
        else:
            if not last:
                raise NotImplementedError("context stream through an attention layer")
            w_qkv = attn_w_qkv[j].astype(BF16)
            qk_gains = jnp.zeros((V7X_SUBLANES, HEAD_DIM), F32).at[0].set(attn_q_norm[j]).at[1].set(attn_k_norm[j])
            rope_tab = _rope_tables(seq)
            qt, k, vt = _qkv(xl, xc, mod, gains[1], w_qkv, qk_gains, rope_tab, bsz)
            mixer = "attn"
            mix_l, mix_c = (_attention(qt, k, vt).reshape(bsz * seq, d),), None
            consts = (attn_w_out[j].astype(BF16),)
            ctx_tile = None

        gains2 = gains[2].at[2].set(norm_post[i, 1])
        xl = _ffn(xl, mod, gains2, w_in2, w_out2, 2, lat_row, mixer, mix_l, consts)
        if not last:
            xc = _ffn(xc, mod, gains2, w_in2, w_out2, 2, ctx_row, mixer, mix_c, consts, ctx_tile)

    return xl.reshape(bsz, seq, d)
```
